```python
import jax, jax.numpy as jnp
from jax import lax
import numpy as np

D_MODEL = 4096
BATCH = 2
SEQ = 4096
DEPTH = 1
DEC_BATCH = 32
DEC_SEQ = 8
PAST_LEN = 8192
PAGE_SIZE = 128

HEAD_DIM = 128
N_HEADS = D_MODEL // HEAD_DIM
H_FOX = N_HEADS // 2
H_MOBA = N_HEADS - H_FOX
W_FOX = H_FOX * HEAD_DIM
W_MOBA = H_MOBA * HEAD_DIM
FOX_Q_BLOCK = 128
MOBA_BLOCK = 256
MOBA_TOPK = 3
MOBA_Q_BLOCK = 64
ROPE_THETA = 10000.0
N_EXPERTS = 32
TOP_K = 4
D_FF = D_MODEL // 2
SWIGLU_LIMIT = 7.0
SWIGLU_ALPHA = 1.702
EXPERT_ROWS = 128
LN_EPS = 1e-5
DEEPNORM_ALPHA = (2.0 * DEPTH) ** 0.25
DEEPNORM_BETA = (8.0 * DEPTH) ** -0.25
SPLIT_SIZES = (W_FOX, W_FOX, W_FOX, H_FOX, W_MOBA, W_MOBA, W_MOBA, D_MODEL, D_MODEL)
D_IN_PROJ = sum(SPLIT_SIZES)

kernel_name = "fox_moba_gated_parallel_moe_decoder_step"


def _q_block(t, blk):
    return blk if t % blk == 0 else t


def layer_norm(x, g, b):
    xf = x.astype(jnp.float32)
    mu = xf.mean(-1, keepdims=True)
    var = jnp.square(xf - mu).mean(-1, keepdims=True)
    return ((xf - mu) * lax.rsqrt(var + LN_EPS) * g.astype(jnp.float32) + b.astype(jnp.float32)).astype(x.dtype)


def rope(x, pos):
    half = HEAD_DIM // 2
    inv = ROPE_THETA ** (-jnp.arange(half, dtype=jnp.float32) / half)
    ang = pos.astype(jnp.float32)[:, None] * inv[None, :]
    cos = jnp.cos(ang)[None, :, None, :]
    sin = jnp.sin(ang)[None, :, None, :]
    xf = x.astype(jnp.float32)
    x1, x2 = xf[..., :half], xf[..., half:]
    return jnp.concatenate([x1 * cos - x2 * sin, x2 * cos + x1 * sin], axis=-1).astype(x.dtype)


def gather_pages(cache, page_table):
    g = cache[page_table]
    return g.reshape((g.shape[0], g.shape[1] * g.shape[2]) + g.shape[3:])


def fox_attention(q, k, v, cum_q, cum_k, q_pos, k_pos):
    B, Tq, H, D = q.shape
    qb = _q_block(Tq, FOX_Q_BLOCK)
    nb = Tq // qb
    scale = D ** -0.5
    ck = cum_k.transpose(0, 2, 1)
    q_blk = q.reshape(B, nb, qb, H, D).transpose(1, 0, 2, 3, 4)
    cq_blk = cum_q.transpose(0, 2, 1).reshape(B, H, nb, qb).transpose(2, 0, 1, 3)
    pos_blk = q_pos.reshape(nb, qb)

    def one_block(args):
        qi, cqi, pi = args
        s = jnp.einsum('bqhd,bkhd->bhqk', qi, k, preferred_element_type=jnp.float32) * scale
        s = s + (cqi[..., :, None] - ck[..., None, :])
        s = jnp.where(pi[:, None] >= k_pos[None, :], s, -jnp.inf)
        p = jax.nn.softmax(s, axis=-1)
        return jnp.einsum('bhqk,bkhd->bqhd', p.astype(v.dtype), v)

    o = lax.map(one_block, (q_blk, cq_blk, pos_blk))
    return o.transpose(1, 0, 2, 3, 4).reshape(B, Tq, H, D)


def moba_attention(q, k, v, q_pos):
    B, Tq, H, D = q.shape
    Tk = k.shape[1]
    n_blk = -(-Tk // MOBA_BLOCK)
    pad = n_blk * MOBA_BLOCK - Tk
    kb = jnp.pad(k, ((0, 0), (0, pad), (0, 0), (0, 0))).reshape(B, n_blk, MOBA_BLOCK, H, D)
    vb = jnp.pad(v, ((0, 0), (0, pad), (0, 0), (0, 0))).reshape(B, n_blk, MOBA_BLOCK, H, D)
    k_mean = kb.astype(jnp.float32).mean(axis=2)
    k_sel = min(MOBA_TOPK, n_blk)
    qc = _q_block(Tq, MOBA_Q_BLOCK)
    nc = Tq // qc
    scale = D ** -0.5
    q_ch = q.reshape(B, nc, qc, H, D).transpose(1, 0, 3, 2, 4)
    pos_ch = q_pos.reshape(nc, qc)
    b_ix = jnp.arange(B)[:, None, None, None]
    h_ix = jnp.arange(H)[None, :, None, None]
    offs = jnp.arange(MOBA_BLOCK)
    blk_ids = jnp.arange(n_blk)

    def one_chunk(args):
        qi, pi = args
        own = pi // MOBA_BLOCK
        gate = jnp.einsum('bhqd,bnhd->bhqn', qi.astype(jnp.float32), k_mean)
        gate = jnp.where(blk_ids[None, :] < own[:, None], gate, -jnp.inf)
        _, sel = lax.top_k(gate, k_sel)
        sel_ok = sel < own[:, None]
        own_b = jnp.broadcast_to(own[:, None], (B, H, qc, 1))
        idx = jnp.concatenate([jnp.where(sel_ok, sel, 0), own_b], axis=-1)
        blk_ok = jnp.concatenate([sel_ok, jnp.ones_like(own_b, dtype=bool)], axis=-1)
        kg = kb[b_ix, idx, :, h_ix]
        vg = vb[b_ix, idx, :, h_ix]
        s = jnp.einsum('bhqd,bhqnld->bhqnl', qi, kg, preferred_element_type=jnp.float32) * scale
        key_pos = idx[..., None] * MOBA_BLOCK + offs
        valid = blk_ok[..., None] & (key_pos <= pi[:, None, None])
        s = jnp.where(valid, s, -jnp.inf)
        p = jax.nn.softmax(s.reshape(B, H, qc, -1), axis=-1).reshape(s.shape)
        return jnp.einsum('bhqnl,bhqnld->bhqd', p.astype(vg.dtype), vg)

    o = lax.map(one_chunk, (q_ch, pos_ch))
    return o.transpose(1, 0, 3, 2, 4).reshape(B, Tq, H, D)


def token_mixers(x, pos, past, w_in, b_forget, w_br_fox, w_br_moba, w_o):
    B, T, _ = x.shape
    proj = jnp.einsum('btd,de->bte', x, w_in)
    points = [int(i) for i in np.cumsum(SPLIT_SIZES)[:-1]]
    q_f, k_f, v_f, f_lin, q_m, k_m, v_m, g_f, g_m = jnp.split(proj, points, axis=-1)
    q_f = q_f.reshape(B, T, H_FOX, HEAD_DIM)
    k_f = k_f.reshape(B, T, H_FOX, HEAD_DIM)
    v_f = v_f.reshape(B, T, H_FOX, HEAD_DIM)
    logf = jax.nn.log_sigmoid(f_lin.astype(jnp.float32) + b_forget.astype(jnp.float32))
    q_m = rope(q_m.reshape(B, T, H_MOBA, HEAD_DIM), pos)
    k_m = rope(k_m.reshape(B, T, H_MOBA, HEAD_DIM), pos)
    v_m = v_m.reshape(B, T, H_MOBA, HEAD_DIM)
    rows = (k_f, v_f, logf, k_m, v_m)
    if past is None:
        kf_all, vf_all, lf_all, km_all, vm_all = rows
    else:
        kf_all, vf_all, lf_all, km_all, vm_all = [jnp.concatenate([p, r], axis=1) for p, r in zip(past, rows)]
    Tk = kf_all.shape[1]
    k_pos = jnp.arange(Tk, dtype=jnp.int32)
    cum = jnp.cumsum(lf_all.astype(jnp.float32), axis=1)
    o_f = fox_attention(q_f, kf_all, vf_all, cum[:, Tk - T:], cum, pos, k_pos)
    o_m = moba_attention(q_m, km_all, vm_all, pos)
    br_f = jnp.einsum('bte,ed->btd', o_f.reshape(B, T, W_FOX), w_br_fox)
    br_m = jnp.einsum('bte,ed->btd', o_m.reshape(B, T, W_MOBA), w_br_moba)
    merged = jax.nn.sigmoid(g_f) * br_f + jax.nn.sigmoid(g_m) * br_m
    return jnp.einsum('btd,de->bte', merged, w_o), rows


def moe_ffn(x, w_router, b_router, w_gate, b_gate, w_up, b_up, w_down, b_down):
    B, T, D = x.shape
    xt = x.reshape(-1, D)
    n_tok = xt.shape[0]
    logits = jnp.einsum('td,de->te', xt, w_router).astype(jnp.float32) + b_router.astype(jnp.float32)
    top_val, top_idx = lax.top_k(logits, TOP_K)
    top_w = jax.nn.softmax(top_val, axis=-1)
    n_assign = n_tok * TOP_K
    flat_e = top_idx.reshape(-1)
    order = jnp.argsort(flat_e)
    sorted_e = flat_e[order]
    sorted_tok = order // TOP_K
    sorted_w = top_w.reshape(-1)[order]
    counts = jnp.bincount(flat_e, length=N_EXPERTS)
    padded = (counts + EXPERT_ROWS - 1) // EXPERT_ROWS * EXPERT_ROWS
    pend = jnp.cumsum(padded)
    pstart = pend - padded
    ustart = jnp.cumsum(counts) - counts
    dest = pstart[sorted_e] + jnp.arange(n_assign) - ustart[sorted_e]
    n_blocks = -(-(n_assign + N_EXPERTS * (EXPERT_ROWS - 1)) // EXPERT_ROWS)
    n_rows = n_blocks * EXPERT_ROWS
    buf_tok = jnp.full((n_rows,), n_tok, jnp.int32).at[dest].set(sorted_tok.astype(jnp.int32))
    x_pad = jnp.concatenate([xt, jnp.zeros((1, D), xt.dtype)], axis=0)
    xbuf = x_pad[buf_tok].reshape(n_blocks, EXPERT_ROWS, D)
    block_e = jnp.minimum(jnp.searchsorted(pend, jnp.arange(n_blocks) * EXPERT_ROWS, side='right'), N_EXPERTS - 1)

    def expert_block(args):
        xb, e = args
        gate = xb @ w_gate[e] + b_gate[e]
        up = xb @ w_up[e] + b_up[e]
        gate = jnp.minimum(gate, SWIGLU_LIMIT)
        up = jnp.clip(up, -SWIGLU_LIMIT, SWIGLU_LIMIT)
        glu = gate * jax.nn.sigmoid(gate * SWIGLU_ALPHA)
        return ((up + 1.0) * glu) @ w_down[e] + b_down[e]

    ybuf = lax.map(expert_block, (xbuf, block_e)).reshape(n_rows, D)
    contrib = ybuf[dest] * sorted_w[:, None].astype(ybuf.dtype)
    y = jnp.zeros((n_tok, D), x.dtype).at[sorted_tok].add(contrib.astype(x.dtype))
    return y.reshape(B, T, D)


def trunk_layer(x, pos, past, lw):
    (w_in, b_forget, w_br_fox, w_br_moba, w_o, ln1_g, ln1_b, w_router, b_router,
     w_gate, b_gate, w_up, b_up, w_down, b_down, ln2_g, ln2_b) = lw
    a, rows = token_mixers(x, pos, past, w_in, b_forget, w_br_fox, w_br_moba, w_o)
    h = layer_norm(DEEPNORM_ALPHA * x + a, ln1_g, ln1_b)
    f = moe_ffn(h, w_router, b_router, w_gate, b_gate, w_up, b_up, w_down, b_down)
    y = layer_norm(DEEPNORM_ALPHA * h + f, ln2_g, ln2_b)
    return y, rows


def setup_inputs(seed: int = 0) -> dict:
    key = jax.random.key(seed)
    ks = jax.random.split(key, 32)
    f32 = jnp.float32
    n_pages = PAST_LEN // PAGE_SIZE
    n_pool = (DEC_BATCH * n_pages * 5) // 4

    def nrm(k, shape, scale=1.0):
        return jax.random.normal(k, shape, f32) * scale

    b_forget = jax.random.uniform(ks[0], (DEPTH, H_FOX), f32, 2.0, 8.0)
    x_prompt = nrm(ks[1], (BATCH, SEQ, D_MODEL))
    x_sample = nrm(ks[2], (DEC_BATCH, DEC_SEQ, D_MODEL))
    cache_fox_k = nrm(ks[3], (DEPTH, n_pool, PAGE_SIZE, H_FOX, HEAD_DIM))
    cache_fox_v = nrm(ks[4], (DEPTH, n_pool, PAGE_SIZE, H_FOX, HEAD_DIM))
    cache_fox_logf = jax.nn.log_sigmoid(b_forget[:, None, None, :] + nrm(ks[5], (DEPTH, n_pool, PAGE_SIZE, H_FOX)))
    cache_moba_k = nrm(ks[6], (DEPTH, n_pool, PAGE_SIZE, H_MOBA, HEAD_DIM))
    cache_moba_v = nrm(ks[7], (DEPTH, n_pool, PAGE_SIZE, H_MOBA, HEAD_DIM))
    perm = jax.random.permutation(ks[8], n_pool)
    page_table = perm[: DEC_BATCH * n_pages].reshape(DEC_BATCH, n_pages).astype(jnp.int32)
    return {
        'x_prompt': x_prompt,
        'x_sample': x_sample,
        'cache_fox_k': cache_fox_k,
        'cache_fox_v': cache_fox_v,
        'cache_fox_logf': cache_fox_logf,
        'cache_moba_k': cache_moba_k,
        'cache_moba_v': cache_moba_v,
        'page_table': page_table,
        'w_in': nrm(ks[9], (DEPTH, D_MODEL, D_IN_PROJ), D_MODEL ** -0.5),
        'b_forget': b_forget,
        'w_br_fox': nrm(ks[10], (DEPTH, W_FOX, D_MODEL), DEEPNORM_BETA * W_FOX ** -0.5),
        'w_br_moba': nrm(ks[11], (DEPTH, W_MOBA, D_MODEL), DEEPNORM_BETA * W_MOBA ** -0.5),
        'w_o': nrm(ks[12], (DEPTH, D_MODEL, D_MODEL), DEEPNORM_BETA * D_MODEL ** -0.5),
        'ln1_g': 1.0 + nrm(ks[13], (DEPTH, D_MODEL), 0.01),
        'ln1_b': nrm(ks[14], (DEPTH, D_MODEL), 0.01),
        'w_router': nrm(ks[15], (DEPTH, D_MODEL, N_EXPERTS), D_MODEL ** -0.5),
        'b_router': nrm(ks[16], (DEPTH, N_EXPERTS), 0.01),
        'w_gate': nrm(ks[17], (DEPTH, N_EXPERTS, D_MODEL, D_FF), D_MODEL ** -0.5),
        'b_gate': nrm(ks[18], (DEPTH, N_EXPERTS, D_FF), 0.01),
        'w_up': nrm(ks[19], (DEPTH, N_EXPERTS, D_MODEL, D_FF), D_MODEL ** -0.5),
        'b_up': nrm(ks[20], (DEPTH, N_EXPERTS, D_FF), 0.01),
        'w_down': nrm(ks[21], (DEPTH, N_EXPERTS, D_FF, D_MODEL), DEEPNORM_BETA * D_FF ** -0.5),
        'b_down': nrm(ks[22], (DEPTH, N_EXPERTS, D_MODEL), 0.01),
        'ln2_g': 1.0 + nrm(ks[23], (DEPTH, D_MODEL), 0.01),
        'ln2_b': nrm(ks[24], (DEPTH, D_MODEL), 0.01),
    }


def reference(x_prompt, x_sample, cache_fox_k, cache_fox_v, cache_fox_logf, cache_moba_k, cache_moba_v,
              page_table, w_in, b_forget, w_br_fox, w_br_moba, w_o, ln1_g, ln1_b, w_router, b_router,
              w_gate, b_gate, w_up, b_up, w_down, b_down, ln2_g, ln2_b):
    past_len = page_table.shape[1] * cache_fox_k.shape[2]
    pos_prompt = jnp.arange(x_prompt.shape[1], dtype=jnp.int32)
    pos_sample = past_len + jnp.arange(x_sample.shape[1], dtype=jnp.int32)
    h_p, h_s = x_prompt, x_sample
    rows_p, rows_s = [], []
    for l in range(DEPTH):
        lw = (w_in[l], b_forget[l], w_br_fox[l], w_br_moba[l], w_o[l], ln1_g[l], ln1_b[l],
              w_router[l], b_router[l], w_gate[l], b_gate[l], w_up[l], b_up[l], w_down[l], b_down[l],
              ln2_g[l], ln2_b[l])
        past = (gather_pages(cache_fox_k[l], page_table), gather_pages(cache_fox_v[l], page_table),
                gather_pages(cache_fox_logf[l], page_table), gather_pages(cache_moba_k[l], page_table),
                gather_pages(cache_moba_v[l], page_table))
        h_p, r_p = trunk_layer(h_p, pos_prompt, None, lw)
        h_s, r_s = trunk_layer(h_s, pos_sample, past, lw)
        rows_p.append(r_p)
        rows_s.append(r_s)

    def stack(rows, i):
        return jnp.stack([r[i] for r in rows])

    return (h_p, h_s,
            stack(rows_p, 0), stack(rows_p, 1), stack(rows_p, 2), stack(rows_p, 3), stack(rows_p, 4),
            stack(rows_s, 0), stack(rows_s, 1), stack(rows_s, 2), stack(rows_s, 3), stack(rows_s, 4))
```

```python
import functools
import math

import jax
import jax.numpy as jnp
from jax import lax
from jax.experimental import pallas as pl
from jax.experimental.pallas import tpu as pltpu

F32 = jnp.float32
BF16 = jnp.bfloat16
I32 = jnp.int32

HEAD_DIM = 128
LANES = 128
MOBA_BLOCK = 256
MOBA_TOPK = 3
ROPE_THETA = 10000.0
TOP_K = 4
SWIGLU_LIMIT = 7.0
SWIGLU_ALPHA = 1.702
LN_EPS = 1e-5
EXPERT_TILE = 256
NEG = -1e30
VMEM_LIMIT = 56 * 1024 * 1024

_NT = (((1,), (1,)), ((), ()))


def _divisor_tile(n, target, mult):
    best = None
    for t in range(mult, min(n, target) + 1, mult):
        if n % t == 0:
            best = t
    return n if best is None else best


def _params(sem):
    return pltpu.CompilerParams(dimension_semantics=sem, vmem_limit_bytes=VMEM_LIMIT)


def _split3(x):
    hi = x.astype(BF16)
    r1 = x - hi.astype(F32)
    mid = r1.astype(BF16)
    lo = (r1 - mid.astype(F32)).astype(BF16)
    return hi, mid, lo


def _layer_norm(u, g, b):
    mu = jnp.mean(u, axis=-1, keepdims=True)
    d = u - mu
    var = jnp.mean(d * d, axis=-1, keepdims=True)
    return d * lax.rsqrt(var + LN_EPS) * g + b


def _topk_lanes(vals, k):
    lane = lax.broadcasted_iota(I32, vals.shape, 1)
    out = []
    for _ in range(k):
        mx = jnp.max(vals, axis=1, keepdims=True)
        first = jnp.min(jnp.where(vals == mx, lane, vals.shape[1]), axis=1, keepdims=True)
        out.append((mx, first))
        vals = jnp.where(lane == first, -jnp.inf, vals)
    return out


def _proj_body(*refs, mode, emit_f32, emit_bf16):
    x_ref, w_ref = refs[0], refs[1]
    rest = list(refs[2:])
    acc = jnp.dot(x_ref[...], w_ref[...], preferred_element_type=F32)
    cos = sin = None
    if mode == "rope":
        cos, sin = rest.pop(0)[...], rest.pop(0)[...]
    elif mode == "logf":
        acc = jax.nn.log_sigmoid(acc + rest.pop(0)[...])
    o32 = rest.pop(0) if emit_f32 else None
    o16 = rest.pop(0) if emit_bf16 else None
    for c in range(acc.shape[1] // HEAD_DIM):
        sl = slice(c * HEAD_DIM, (c + 1) * HEAD_DIM)
        blk = acc[:, sl]
        if mode == "rope":
            blk = blk * cos + pltpu.roll(blk, HEAD_DIM // 2, 1) * sin
        if emit_f32:
            o32[:, sl] = blk
        if emit_bf16:
            o16[:, sl] = blk.astype(BF16)


def _proj(x, w, n0, n, *, tn, mode="plain", tables=None, bias=None, emit_f32=False, emit_bf16=False, name):
    m, k = x.shape
    tm = _divisor_tile(m, 1056, 16)
    assert n0 % tn == 0 and n % tn == 0
    j0 = n0 // tn
    in_specs = [pl.BlockSpec((tm, k), lambda i, j: (i, 0)), pl.BlockSpec((k, tn), lambda i, j: (0, j + j0))]
    args = [x, w]
    if mode == "rope":
        in_specs += [pl.BlockSpec((tm, HEAD_DIM), lambda i, j: (i, 0))] * 2
        args += list(tables)
    elif mode == "logf":
        in_specs.append(pl.BlockSpec((1, tn), lambda i, j: (0, j)))
        args.append(bias)
    out_shape, out_specs = [], []
    for flag, dt in ((emit_f32, F32), (emit_bf16, BF16)):
        if flag:
            out_shape.append(jax.ShapeDtypeStruct((m, n), dt))
            out_specs.append(pl.BlockSpec((tm, tn), lambda i, j: (i, j)))
    return pl.pallas_call(
        functools.partial(_proj_body, mode=mode, emit_f32=emit_f32, emit_bf16=emit_bf16),
        out_shape=out_shape, grid=(m // tm, n // tn), in_specs=in_specs, out_specs=out_specs,
        compiler_params=_params(("parallel", "arbitrary")), name=name)(*args)


def _cumsum_body(x_ref, o_ref, carry, *, tc):
    @pl.when(pl.program_id(1) == 0)
    def _():
        carry[...] = jnp.zeros_like(carry)
    x = x_ref[...]
    row = lax.broadcasted_iota(I32, (tc, tc), 0)
    col = lax.broadcasted_iota(I32, (tc, tc), 1)
    tri = (row >= col).astype(BF16)
    y = carry[...]
    for piece in _split3(x):
        y = y + jnp.dot(tri, piece, preferred_element_type=F32)
    o_ref[...] = y
    carry[...] = carry[...] + jnp.sum(x, axis=0, keepdims=True)


def _cumsum_rows(x, nb, t):
    tc = _divisor_tile(t, 512, 8)
    nc = t // tc
    return pl.pallas_call(
        functools.partial(_cumsum_body, tc=tc),
        out_shape=jax.ShapeDtypeStruct(x.shape, F32), grid=(nb, nc),
        in_specs=[pl.BlockSpec((tc, LANES), lambda b, c: (b * nc + c, 0))],
        out_specs=pl.BlockSpec((tc, LANES), lambda b, c: (b * nc + c, 0)),
        scratch_shapes=[pltpu.VMEM((1, LANES), F32)],
        compiler_params=_params(("parallel", "arbitrary")), name="cumsum_rows")(x)


def _decay_body(pt_ref, x_ref, o_ref, carry, *, page):
    @pl.when(pl.program_id(1) == 0)
    def _():
        carry[...] = jnp.zeros_like(carry)
    x = x_ref[0]
    row = lax.broadcasted_iota(I32, (page, page), 0)
    col = lax.broadcasted_iota(I32, (page, page), 1)
    tri = (col > row).astype(BF16)
    y = jnp.broadcast_to(carry[...], x.shape)
    for piece in _split3(x):
        y = y + jnp.dot(tri, piece, preferred_element_type=F32)
    o_ref[0, 0] = y
    carry[...] = carry[...] + jnp.sum(x, axis=0, keepdims=True)


def _past_decay(cache_logf, page_table):
    _, page, h = cache_logf.shape
    nb, n_pages = page_table.shape
    grid_spec = pltpu.PrefetchScalarGridSpec(
        num_scalar_prefetch=1, grid=(nb, n_pages),
        in_specs=[pl.BlockSpec((1, page, h), lambda b, j, pt: (pt[b, n_pages - 1 - j], 0, 0))],
        out_specs=pl.BlockSpec((1, 1, page, h), lambda b, j, pt: (b, n_pages - 1 - j, 0, 0)),
        scratch_shapes=[pltpu.VMEM((1, h), F32)])
    return pl.pallas_call(
        functools.partial(_decay_body, page=page),
        out_shape=jax.ShapeDtypeStruct((nb, n_pages, page, h), F32), grid_spec=grid_spec,
        compiler_params=_params(("parallel", "arbitrary")), name="past_decay")(page_table, cache_logf)


def _online_update(z, v, m_sc, l_sc, acc_sc):
    m_old = m_sc[...]
    m_new = jnp.maximum(m_old, jnp.max(z, axis=1, keepdims=True))
    alpha = jnp.exp(m_old - m_new)
    p = jnp.exp(z - m_new)
    l_sc[...] = alpha * l_sc[...] + jnp.sum(p, axis=1, keepdims=True)
    acc_sc[...] = alpha * acc_sc[...] + jnp.dot(p.astype(BF16), v, preferred_element_type=F32)
    m_sc[...] = m_new


def _init_softmax(m_sc, l_sc, acc_sc):
    m_sc[...] = jnp.full_like(m_sc, -jnp.inf)
    l_sc[...] = jnp.zeros_like(l_sc)
    acc_sc[...] = jnp.zeros_like(acc_sc)


def _fox_prompt_body(q_ref, k_ref, v_ref, cum_ref, cumt_ref, o_ref, m_sc, l_sc, acc_sc, cq_sc, *, scale, tq):
    h, qi, ki = pl.program_id(1), pl.program_id(2), pl.program_id(3)

    @pl.when(ki == 0)
    def _():
        _init_softmax(m_sc, l_sc, acc_sc)
        lane = lax.broadcasted_iota(I32, cum_ref.shape, 1)
        cq_sc[...] = jnp.sum(jnp.where(lane == h, cum_ref[...], 0.0), axis=1, keepdims=True)

    @pl.when(ki <= qi)
    def _():
        s = lax.dot_general(q_ref[...], k_ref[...], _NT, preferred_element_type=F32) * scale
        s = s + (cq_sc[...] - cumt_ref[0, pl.ds(h, 1), :])
        row = lax.broadcasted_iota(I32, s.shape, 0) + qi * tq
        col = lax.broadcasted_iota(I32, s.shape, 1) + ki * tq
        _online_update(jnp.where(row >= col, s, -jnp.inf), v_ref[...], m_sc, l_sc, acc_sc)

    @pl.when(ki == qi)
    def _():
        o_ref[...] = (acc_sc[...] / l_sc[...]).astype(o_ref.dtype)


def _fox_prompt(q, k, v, cum, cum_t, nb, t, n_heads):
    tq = _divisor_tile(t, 512, 16)
    nq = t // tq
    kv_map = lambda b, h, qi, ki: (b * nq + jnp.minimum(ki, qi), h)
    return pl.pallas_call(
        functools.partial(_fox_prompt_body, scale=HEAD_DIM ** -0.5, tq=tq),
        out_shape=jax.ShapeDtypeStruct((nb * t, n_heads * HEAD_DIM), BF16),
        grid=(nb, n_heads, nq, nq),
        in_specs=[pl.BlockSpec((tq, HEAD_DIM), lambda b, h, qi, ki: (b * nq + qi, h)),
                  pl.BlockSpec((tq, HEAD_DIM), kv_map),
                  pl.BlockSpec((tq, HEAD_DIM), kv_map),
                  pl.BlockSpec((tq, LANES), lambda b, h, qi, ki: (b * nq + qi, 0)),
                  pl.BlockSpec((1, n_heads, tq), lambda b, h, qi, ki: (b, 0, jnp.minimum(ki, qi)))],
        out_specs=pl.BlockSpec((tq, HEAD_DIM), lambda b, h, qi, ki: (b * nq + qi, h)),
        scratch_shapes=[pltpu.VMEM((tq, 1), F32), pltpu.VMEM((tq, 1), F32), pltpu.VMEM((tq, HEAD_DIM), F32),
                        pltpu.VMEM((tq, 1), F32)],
        compiler_params=_params(("parallel", "parallel", "parallel", "arbitrary")), name="fox_prompt")(q, k, v, cum, cum_t)


def _block_mean_body(k_ref, o_ref, *, blk):
    o_ref[0] = jnp.sum(k_ref[...], axis=0, keepdims=True) * (1.0 / blk)


def _block_means(k, n_blocks):
    w = k.shape[1]
    return pl.pallas_call(
        functools.partial(_block_mean_body, blk=MOBA_BLOCK),
        out_shape=jax.ShapeDtypeStruct((n_blocks, 1, w), F32), grid=(n_blocks,),
        in_specs=[pl.BlockSpec((MOBA_BLOCK, w), lambda n: (n, 0))],
        out_specs=pl.BlockSpec((1, 1, w), lambda n: (n, 0, 0)),
        compiler_params=_params(("parallel",)), name="moba_block_means")(k)


def _select_blocks(gate, n_valid):
    lane = lax.broadcasted_iota(I32, gate.shape, 1)
    picks = _topk_lanes(jnp.where(lane < n_valid, gate, -jnp.inf), MOBA_TOPK)
    sel = jnp.zeros(gate.shape, F32)
    for mx, first in picks:
        sel = jnp.where((lane == first) & (mx > -jnp.inf), 1.0, sel)
    return sel


def _moba_prompt_body(q_ref, k_ref, v_ref, km_ref, o_ref, m_sc, l_sc, acc_sc, sel_sc, *, scale):
    qi, kj = pl.program_id(2), pl.program_id(3)

    def scores():
        return lax.dot_general(q_ref[...], k_ref[...], _NT, preferred_element_type=F32) * scale

    @pl.when(kj == 0)
    def _():
        _init_softmax(m_sc, l_sc, acc_sc)
        q = q_ref[...]
        gate = sum(lax.dot_general(q, piece, _NT, preferred_element_type=F32) for piece in _split3(km_ref[0]))
        sel_sc[...] = _select_blocks(gate, qi)
        s = scores()
        row = lax.broadcasted_iota(I32, s.shape, 0)
        col = lax.broadcasted_iota(I32, s.shape, 1)
        _online_update(jnp.where(row >= col, s, NEG), v_ref[...], m_sc, l_sc, acc_sc)

    @pl.when((kj > 0) & (kj <= qi))
    def _():
        lane = lax.broadcasted_iota(I32, sel_sc.shape, 1)
        picked = jnp.max(jnp.where(lane == qi - kj, sel_sc[...], 0.0), axis=1, keepdims=True)
        _online_update(jnp.where(picked > 0.0, scores(), NEG), v_ref[...], m_sc, l_sc, acc_sc)

    @pl.when(kj == qi)
    def _():
        o_ref[...] = (acc_sc[...] / l_sc[...]).astype(o_ref.dtype)


def _moba_prompt(q, k, v, kmean, nb, t, n_heads):
    blk = MOBA_BLOCK
    nq = t // blk
    kv_map = lambda b, h, qi, kj: (b * nq + jnp.maximum(qi - kj, 0), h)
    return pl.pallas_call(
        functools.partial(_moba_prompt_body, scale=HEAD_DIM ** -0.5),
        out_shape=jax.ShapeDtypeStruct((nb * t, n_heads * HEAD_DIM), BF16),
        grid=(nb, n_heads, nq, nq),
        in_specs=[pl.BlockSpec((blk, HEAD_DIM), lambda b, h, qi, kj: (b * nq + qi, h)),
                  pl.BlockSpec((blk, HEAD_DIM), kv_map),
                  pl.BlockSpec((blk, HEAD_DIM), kv_map),
                  pl.BlockSpec((1, LANES, HEAD_DIM), lambda b, h, qi, kj: (b, 0, h))],
        out_specs=pl.BlockSpec((blk, HEAD_DIM), lambda b, h, qi, kj: (b * nq + qi, h)),
        scratch_shapes=[pltpu.VMEM((blk, 1), F32), pltpu.VMEM((blk, 1), F32), pltpu.VMEM((blk, HEAD_DIM), F32),
                        pltpu.VMEM((blk, LANES), F32)],
        compiler_params=_params(("parallel", "parallel", "parallel", "arbitrary")), name="moba_prompt")(q, k, v, kmean)


def _head_masks(n_heads, tq, page):
    rows = jnp.arange(n_heads * tq)[:, None]
    cols = jnp.arange(page * n_heads)[None, :]
    past = jnp.where(rows // tq == cols % n_heads, 0.0, NEG).astype(F32)
    ncol = jnp.arange(tq * n_heads)[None, :]
    new = jnp.where((rows // tq == ncol % n_heads) & (ncol // n_heads <= rows % tq), 0.0, NEG).astype(F32)
    return past, new


def _fox_sample_body(pt_ref, q_ref, k_ref, v_ref, d_ref, cq_ref, pm_ref, kn_ref, vn_ref, dn_ref, nm_ref,
                     o_ref, m_sc, l_sc, acc_sc, *, scale, n_pages):
    j = pl.program_id(1)

    @pl.when(j == 0)
    def _():
        _init_softmax(m_sc, l_sc, acc_sc)

    cq = cq_ref[0][:, :1]

    @pl.when(j < n_pages)
    def _():
        s = lax.dot_general(q_ref[0], k_ref[0].astype(BF16), _NT, preferred_element_type=F32)
        z = s * scale + pm_ref[...] + (d_ref[0, 0] + cq)
        _online_update(z, v_ref[0].astype(BF16), m_sc, l_sc, acc_sc)

    @pl.when(j == n_pages)
    def _():
        s = lax.dot_general(q_ref[0], kn_ref[0], _NT, preferred_element_type=F32)
        z = s * scale + nm_ref[...] + (cq - dn_ref[0])
        _online_update(z, vn_ref[0], m_sc, l_sc, acc_sc)
        o_ref[0] = acc_sc[...] / l_sc[...]


def _fox_sample(page_table, q_all, cache_k, cache_v, decay_flat, cq_b, k_new, v_new, cum_new_flat, masks):
    nb, r, _ = q_all.shape
    n_pages = page_table.shape[1]
    pw = cache_k.shape[1]
    nw = k_new.shape[1]
    pmask, nmask = masks
    page_map = lambda b, j, pt: (pt[b, jnp.minimum(j, n_pages - 1)], 0, 0)
    per_b = lambda b, j, pt: (b, 0, 0)
    const = lambda b, j, pt: (0, 0)
    grid_spec = pltpu.PrefetchScalarGridSpec(
        num_scalar_prefetch=1, grid=(nb, n_pages + 1),
        in_specs=[pl.BlockSpec((1, r, HEAD_DIM), per_b),
                  pl.BlockSpec((1, pw, HEAD_DIM), page_map),
                  pl.BlockSpec((1, pw, HEAD_DIM), page_map),
                  pl.BlockSpec((1, 1, 1, pw), lambda b, j, pt: (b, jnp.minimum(j, n_pages - 1), 0, 0)),
                  pl.BlockSpec((1, r, LANES), per_b),
                  pl.BlockSpec((r, pw), const),
                  pl.BlockSpec((1, nw, HEAD_DIM), per_b),
                  pl.BlockSpec((1, nw, HEAD_DIM), per_b),
                  pl.BlockSpec((1, 1, nw), per_b),
                  pl.BlockSpec((r, nw), const)],
        out_specs=pl.BlockSpec((1, r, HEAD_DIM), per_b),
        scratch_shapes=[pltpu.VMEM((r, 1), F32), pltpu.VMEM((r, 1), F32), pltpu.VMEM((r, HEAD_DIM), F32)])
    return pl.pallas_call(
        functools.partial(_fox_sample_body, scale=HEAD_DIM ** -0.5, n_pages=n_pages),
        out_shape=jax.ShapeDtypeStruct((nb, r, HEAD_DIM), F32), grid_spec=grid_spec,
        compiler_params=_params(("parallel", "arbitrary")), name="fox_sample")(
            page_table, q_all, cache_k, cache_v, decay_flat, cq_b, pmask, k_new, v_new, cum_new_flat, nmask)


def _moba_sample_body(pt_ref, q_ref, *refs, scale, n_past, ppb, n_heads, tq):
    k_refs, v_refs = refs[:ppb], refs[ppb:2 * ppb]
    pm_ref, kn_ref, vn_ref, nm_ref, o_ref, accs, gate_sc, mx_sc, l_sc = refs[2 * ppb:]
    n = pl.program_id(1)
    q = q_ref[0]
    r = q.shape[0]
    lane = lax.broadcasted_iota(I32, (r, LANES), 1)

    def block_softmax(ks, vs, mask):
        zs = [lax.dot_general(q, kb, _NT, preferred_element_type=F32) * scale + mask for kb in ks]
        m = functools.reduce(jnp.maximum, [jnp.max(z, axis=1, keepdims=True) for z in zs])
        ps = [jnp.exp(z - m) for z in zs]
        l = sum(jnp.sum(p, axis=1, keepdims=True) for p in ps)
        acc = sum(jnp.dot(p.astype(BF16), vb, preferred_element_type=F32) for p, vb in zip(ps, vs))
        return m, l, acc

    @pl.when(n == 0)
    def _():
        gate_sc[...] = jnp.zeros_like(gate_sc)
        mx_sc[...] = jnp.zeros_like(mx_sc)
        l_sc[...] = jnp.zeros_like(l_sc)

    @pl.when(n < n_past)
    def _():
        kf = [kr[0] for kr in k_refs]
        m, l, acc = block_softmax([x.astype(BF16) for x in kf], [vr[0].astype(BF16) for vr in v_refs], pm_ref[...])
        accs[n] = acc
        tok = kf[0].shape[0] // n_heads
        ksum = sum(jnp.sum(x.reshape(tok, n_heads, HEAD_DIM), axis=0) for x in kf)
        kmean = ksum * (1.0 / (tok * ppb))
        kexp = jnp.broadcast_to(kmean[:, None, :], (n_heads, tq, HEAD_DIM)).reshape(r, HEAD_DIM)
        gate = jnp.sum(q.astype(F32) * kexp, axis=1, keepdims=True)
        gate_sc[...] = jnp.where(lane == n, gate, gate_sc[...])
        mx_sc[...] = jnp.where(lane == n, m, mx_sc[...])
        l_sc[...] = jnp.where(lane == n, l, l_sc[...])

    @pl.when(n == n_past)
    def _():
        m_own, l_own, acc_own = block_softmax([kn_ref[0]], [vn_ref[0]], nm_ref[...])
        sel = _select_blocks(gate_sc[...], n_past)
        m_all = jnp.maximum(jnp.max(jnp.where(sel > 0.0, mx_sc[...], -jnp.inf), axis=1, keepdims=True), m_own)
        wn = jnp.where(sel > 0.0, jnp.exp(mx_sc[...] - m_all), 0.0)
        w_own = jnp.exp(m_own - m_all)
        denom = jnp.sum(wn * l_sc[...], axis=1, keepdims=True) + w_own * l_own

        def add_block(i, num):
            wi = jnp.sum(jnp.where(lane == i, wn, 0.0), axis=1, keepdims=True)
            return num + wi * accs[i]

        num = lax.fori_loop(0, n_past, add_block, w_own * acc_own)
        o_ref[0] = num / denom


def _moba_sample(page_table, q_all, cache_k, cache_v, k_new, v_new, masks, n_heads, tq):
    nb, r, _ = q_all.shape
    n_pages = page_table.shape[1]
    pw = cache_k.shape[1]
    nw = k_new.shape[1]
    page = pw // n_heads
    ppb = MOBA_BLOCK // page
    n_past = n_pages // ppb
    pmask, nmask = masks

    def page_map(p):
        return lambda b, n, pt: (pt[b, jnp.minimum(n, n_past - 1) * ppb + p], 0, 0)

    per_b = lambda b, n, pt: (b, 0, 0)
    const = lambda b, n, pt: (0, 0)
    page_specs = [pl.BlockSpec((1, pw, HEAD_DIM), page_map(p)) for p in range(ppb)]
    grid_spec = pltpu.PrefetchScalarGridSpec(
        num_scalar_prefetch=1, grid=(nb, n_past + 1),
        in_specs=[pl.BlockSpec((1, r, HEAD_DIM), per_b)] + page_specs + page_specs +
                 [pl.BlockSpec((r, pw), const),
                  pl.BlockSpec((1, nw, HEAD_DIM), per_b),
                  pl.BlockSpec((1, nw, HEAD_DIM), per_b),
                  pl.BlockSpec((r, nw), const)],
        out_specs=pl.BlockSpec((1, r, HEAD_DIM), per_b),
        scratch_shapes=[pltpu.VMEM((n_past, r, HEAD_DIM), F32), pltpu.VMEM((r, LANES), F32),
                        pltpu.VMEM((r, LANES), F32), pltpu.VMEM((r, LANES), F32)])
    return pl.pallas_call(
        functools.partial(_moba_sample_body, scale=HEAD_DIM ** -0.5, n_past=n_past, ppb=ppb, n_heads=n_heads, tq=tq),
        out_shape=jax.ShapeDtypeStruct((nb, r, HEAD_DIM), F32), grid_spec=grid_spec,
        compiler_params=_params(("parallel", "arbitrary")), name="moba_sample")(
            page_table, q_all, *([cache_k] * ppb), *([cache_v] * ppb), pmask, k_new, v_new, nmask)


def _merge_body(of_ref, om_ref, wf_ref, wm_ref, gf_ref, gm_ref, o_ref):
    br_f = jnp.dot(of_ref[...], wf_ref[...], preferred_element_type=F32)
    br_m = jnp.dot(om_ref[...], wm_ref[...], preferred_element_type=F32)
    o_ref[...] = (jax.nn.sigmoid(gf_ref[...]) * br_f + jax.nn.sigmoid(gm_ref[...]) * br_m).astype(o_ref.dtype)


def _merge(o_f, o_m, w_f, w_m, gates, d):
    m = o_f.shape[0]
    tm = _divisor_tile(m, 1056, 16)
    tn = _divisor_tile(d, 512, LANES)
    nj = d // tn
    return pl.pallas_call(
        _merge_body, out_shape=jax.ShapeDtypeStruct((m, d), BF16), grid=(m // tm, nj),
        in_specs=[pl.BlockSpec((tm, o_f.shape[1]), lambda i, j: (i, 0)),
                  pl.BlockSpec((tm, o_m.shape[1]), lambda i, j: (i, 0)),
                  pl.BlockSpec((w_f.shape[0], tn), lambda i, j: (0, j)),
                  pl.BlockSpec((w_m.shape[0], tn), lambda i, j: (0, j)),
                  pl.BlockSpec((tm, tn), lambda i, j: (i, j)),
                  pl.BlockSpec((tm, tn), lambda i, j: (i, j + nj))],
        out_specs=pl.BlockSpec((tm, tn), lambda i, j: (i, j)),
        compiler_params=_params(("parallel", "arbitrary")), name="branch_merge")(o_f, o_m, w_f, w_m, gates, gates)


def _out_ln_body(a_ref, w_ref, x_ref, g_ref, b_ref, wrh_ref, wrl_ref, br_ref, h_ref, idx_ref, wgt_ref, acc,
                 *, alpha, n_experts):
    k = pl.program_id(1)

    @pl.when(k == 0)
    def _():
        acc[...] = jnp.zeros_like(acc)

    acc[...] += jnp.dot(a_ref[...], w_ref[...], preferred_element_type=F32)

    @pl.when(k == pl.num_programs(1) - 1)
    def _():
        h = _layer_norm(alpha * x_ref[...] + acc[...], g_ref[...], b_ref[...])
        h_ref[...] = h
        h_hi = h.astype(BF16)
        h_lo = (h - h_hi.astype(F32)).astype(BF16)
        logits = (jnp.dot(h_hi, wrh_ref[...], preferred_element_type=F32)
                  + jnp.dot(h_hi, wrl_ref[...], preferred_element_type=F32)
                  + jnp.dot(h_lo, wrh_ref[...], preferred_element_type=F32)) + br_ref[...]
        lane = lax.broadcasted_iota(I32, logits.shape, 1)
        picks = _topk_lanes(jnp.where(lane < n_experts, logits, -jnp.inf), TOP_K)
        idx = jnp.zeros(logits.shape, I32)
        e = jnp.zeros(logits.shape, F32)
        for r, (mx, first) in enumerate(picks):
            idx = jnp.where(lane == r, first, idx)
            e = jnp.where(lane == r, jnp.exp(mx - picks[0][0]), e)
        idx_ref[...] = idx
        wgt_ref[...] = e / jnp.sum(e, axis=1, keepdims=True)


def _out_ln_route(merged, w_o, x, ln_g, ln_b, wr_hi, wr_lo, b_r, alpha, n_experts):
    m, d = x.shape
    tm = _divisor_tile(m, 264, 8)
    tk = _divisor_tile(d, 1024, LANES)
    row = lambda i, k: (i, 0)
    const = lambda i, k: (0, 0)
    return pl.pallas_call(
        functools.partial(_out_ln_body, alpha=alpha, n_experts=n_experts),
        out_shape=[jax.ShapeDtypeStruct((m, d), F32), jax.ShapeDtypeStruct((m, LANES), I32),
                   jax.ShapeDtypeStruct((m, LANES), F32)],
        grid=(m // tm, d // tk),
        in_specs=[pl.BlockSpec((tm, tk), lambda i, k: (i, k)), pl.BlockSpec((tk, d), lambda i, k: (k, 0)),
                  pl.BlockSpec((tm, d), row), pl.BlockSpec((1, d), const), pl.BlockSpec((1, d), const),
                  pl.BlockSpec((d, LANES), const), pl.BlockSpec((d, LANES), const), pl.BlockSpec((1, LANES), const)],
        out_specs=[pl.BlockSpec((tm, d), row), pl.BlockSpec((tm, LANES), row), pl.BlockSpec((tm, LANES), row)],
        scratch_shapes=[pltpu.VMEM((tm, d), F32)],
        compiler_params=_params(("parallel", "arbitrary")), name="out_proj_ln_router")(
            merged, w_o, x, ln_g, ln_b, wr_hi, wr_lo, b_r)


def _gather_rows_body(tok_ref, h_hbm, o_ref, buf, sem, *, tm):
    base = pl.program_id(0) * tm

    def issue(r, c):
        pltpu.make_async_copy(h_hbm.at[pl.ds(tok_ref[base + r], 1)], buf.at[pl.ds(r, 1)], sem).start()
        return c

    lax.fori_loop(0, tm, issue, 0)
    pltpu.make_async_copy(h_hbm.at[pl.ds(0, tm)], buf, sem).wait()
    o_ref[...] = buf[...].astype(o_ref.dtype)


def _gather_rows(h, buf_tok, tm):
    n_rows = buf_tok.shape[0]
    d = h.shape[1]
    grid_spec = pltpu.PrefetchScalarGridSpec(
        num_scalar_prefetch=1, grid=(n_rows // tm,),
        in_specs=[pl.BlockSpec(memory_space=pl.ANY)],
        out_specs=pl.BlockSpec((tm, d), lambda i, tok: (i, 0)),
        scratch_shapes=[pltpu.VMEM((tm, d), F32), pltpu.SemaphoreType.DMA(())])
    return pl.pallas_call(
        functools.partial(_gather_rows_body, tm=tm),
        out_shape=jax.ShapeDtypeStruct((n_rows, d), BF16), grid_spec=grid_spec,
        compiler_params=_params(("arbitrary",)), name="moe_gather_rows")(buf_tok, h)


def _expert_changed(te_ref, i):
    return (i == 0) | (te_ref[i] != te_ref[jnp.maximum(i - 1, 0)])


def _gate_up_body(te_ref, nu_ref, x_ref, wg_ref, wu_ref, bg_ref, bu_ref, o_ref, wg_bf, wu_bf):
    i = pl.program_id(1)
    used = i < nu_ref[0]

    @pl.when(used & _expert_changed(te_ref, i))
    def _():
        wg_bf[...] = wg_ref[0].astype(BF16)
        wu_bf[...] = wu_ref[0].astype(BF16)

    @pl.when(used)
    def _():
        x = x_ref[...]
        gate = jnp.dot(x, wg_bf[...], preferred_element_type=F32) + bg_ref[0]
        up = jnp.dot(x, wu_bf[...], preferred_element_type=F32) + bu_ref[0]
        gate = jnp.minimum(gate, SWIGLU_LIMIT)
        up = jnp.clip(up, -SWIGLU_LIMIT, SWIGLU_LIMIT)
        glu = gate * jax.nn.sigmoid(gate * SWIGLU_ALPHA)
        o_ref[...] = ((up + 1.0) * glu).astype(o_ref.dtype)

    @pl.when(jnp.logical_not(used))
    def _():
        o_ref[...] = jnp.zeros_like(o_ref)


def _gate_up(xs, tile_e, n_used, w_gate, b_gate, w_up, b_up, tm):
    n_rows, d = xs.shape
    f = w_gate.shape[2]
    tf = _divisor_tile(f, 512, LANES)
    wmap = lambda j, i, te, nu: (te[i], 0, j)
    grid_spec = pltpu.PrefetchScalarGridSpec(
        num_scalar_prefetch=2, grid=(f // tf, n_rows // tm),
        in_specs=[pl.BlockSpec((tm, d), lambda j, i, te, nu: (i, 0)),
                  pl.BlockSpec((1, d, tf), wmap), pl.BlockSpec((1, d, tf), wmap),
                  pl.BlockSpec((1, 1, tf), wmap), pl.BlockSpec((1, 1, tf), wmap)],
        out_specs=pl.BlockSpec((tm, tf), lambda j, i, te, nu: (i, j)),
        scratch_shapes=[pltpu.VMEM((d, tf), BF16), pltpu.VMEM((d, tf), BF16)])
    return pl.pallas_call(
        _gate_up_body, out_shape=jax.ShapeDtypeStruct((n_rows, f), BF16), grid_spec=grid_spec,
        compiler_params=_params(("arbitrary", "arbitrary")), name="moe_gate_up")(
            tile_e, n_used, xs, w_gate, w_up, b_gate, b_up)


def _down_body(te_ref, nu_ref, a_ref, w_ref, b_ref, o_ref, w_bf):
    i = pl.program_id(1)
    used = i < nu_ref[0]

    @pl.when(used & _expert_changed(te_ref, i))
    def _():
        w_bf[...] = w_ref[0].astype(BF16)

    @pl.when(used)
    def _():
        o_ref[...] = jnp.dot(a_ref[...], w_bf[...], preferred_element_type=F32) + b_ref[0]

    @pl.when(jnp.logical_not(used))
    def _():
        o_ref[...] = jnp.zeros_like(o_ref)


def _down(act, tile_e, n_used, w_down, b_down, tm):
    n_rows, f = act.shape
    d = w_down.shape[2]
    tn = _divisor_tile(d, 1024, LANES)
    wmap = lambda j, i, te, nu: (te[i], 0, j)
    grid_spec = pltpu.PrefetchScalarGridSpec(
        num_scalar_prefetch=2, grid=(d // tn, n_rows // tm),
        in_specs=[pl.BlockSpec((tm, f), lambda j, i, te, nu: (i, 0)),
                  pl.BlockSpec((1, f, tn), wmap), pl.BlockSpec((1, 1, tn), wmap)],
        out_specs=pl.BlockSpec((tm, tn), lambda j, i, te, nu: (i, j)),
        scratch_shapes=[pltpu.VMEM((f, tn), BF16)])
    return pl.pallas_call(
        _down_body, out_shape=jax.ShapeDtypeStruct((n_rows, d), F32), grid_spec=grid_spec,
        compiler_params=_params(("arbitrary", "arbitrary")), name="moe_down")(tile_e, n_used, act, w_down, b_down)


def _combine_ln_body(pos_ref, y_hbm, h_ref, w_ref, g_ref, b_ref, o_ref, buf, sem, *, tc, alpha):
    base = pl.program_id(0) * tc * TOP_K

    def issue(r, c):
        for k in range(TOP_K):
            pltpu.make_async_copy(y_hbm.at[pl.ds(pos_ref[base + r * TOP_K + k], 1)], buf.at[k, pl.ds(r, 1)], sem).start()
        return c

    lax.fori_loop(0, tc, issue, 0)
    for k in range(TOP_K):
        pltpu.make_async_copy(y_hbm.at[pl.ds(0, tc)], buf.at[k], sem).wait()
    w = w_ref[...]
    f = sum(w[:, k:k + 1] * buf[k] for k in range(TOP_K))
    o_ref[...] = _layer_norm(alpha * h_ref[...] + f, g_ref[...], b_ref[...])


def _combine_ln(y_rows, pos, h, w_pad, ln_g, ln_b, alpha):
    m, d = h.shape
    tc = _divisor_tile(m, 128, 8)
    row = lambda i, p: (i, 0)
    const = lambda i, p: (0, 0)
    grid_spec = pltpu.PrefetchScalarGridSpec(
        num_scalar_prefetch=1, grid=(m // tc,),
        in_specs=[pl.BlockSpec(memory_space=pl.ANY), pl.BlockSpec((tc, d), row), pl.BlockSpec((tc, LANES), row),
                  pl.BlockSpec((1, d), const), pl.BlockSpec((1, d), const)],
        out_specs=pl.BlockSpec((tc, d), row),
        scratch_shapes=[pltpu.VMEM((TOP_K, tc, d), F32), pltpu.SemaphoreType.DMA(())])
    return pl.pallas_call(
        functools.partial(_combine_ln_body, tc=tc, alpha=alpha),
        out_shape=jax.ShapeDtypeStruct((m, d), F32), grid_spec=grid_spec,
        compiler_params=_params(("arbitrary",)), name="moe_combine_ln")(pos, y_rows, h, w_pad, ln_g, ln_b)


def _moe_ln(h, top_idx, w_pad, w_gate, b_gate, w_up, b_up, w_down, b_down, ln_g, ln_b, alpha):
    m, d = h.shape
    n_experts = w_gate.shape[0]
    tm = EXPERT_TILE
    n_assign = m * TOP_K
    flat_e = top_idx.reshape(-1)
    order = jnp.argsort(flat_e)
    sorted_e = flat_e[order]
    counts = jnp.bincount(flat_e, length=n_experts)
    padded = (counts + tm - 1) // tm * tm
    pend = jnp.cumsum(padded)
    dest = (pend - padded)[sorted_e] + jnp.arange(n_assign) - (jnp.cumsum(counts) - counts)[sorted_e]
    n_tiles = -(-(n_assign + n_experts * (tm - 1)) // tm)
    buf_tok = jnp.zeros((n_tiles * tm,), I32).at[dest].set((order // TOP_K).astype(I32))
    tile_e = jnp.minimum(jnp.searchsorted(pend, jnp.arange(n_tiles) * tm, side="right"), n_experts - 1).astype(I32)
    n_used = (pend[-1] // tm).astype(I32).reshape(1)
    pos = jnp.zeros((n_assign,), I32).at[order].set(dest.astype(I32))

    xs = _gather_rows(h, buf_tok, tm)
    act = _gate_up(xs, tile_e, n_used, w_gate, b_gate[:, None, :], w_up, b_up[:, None, :], tm)
    y_rows = _down(act, tile_e, n_used, w_down, b_down[:, None, :], tm)
    return _combine_ln(y_rows, pos, h, w_pad, ln_g, ln_b, alpha)


def _rope_tables(pos):
    half = HEAD_DIM // 2
    inv = ROPE_THETA ** (-jnp.arange(half, dtype=F32) / half)
    ang = pos.astype(F32)[:, None] * inv[None, :]
    cos, sin = jnp.cos(ang), jnp.sin(ang)
    return jnp.concatenate([cos, cos], axis=1), jnp.concatenate([-sin, sin], axis=1)


def _layer(x_all, dims, caches, page_table, lw, alpha):
    (w_in, b_forget, w_br_fox, w_br_moba, w_o, ln1_g, ln1_b, w_router, b_router,
     w_gate, b_gate, w_up, b_up, w_down, b_down, ln2_g, ln2_b) = lw
    nb_p, t_p, nb_s, t_s = dims
    cache_fk, cache_fv, cache_flog, cache_mk, cache_mv = caches
    m, d = x_all.shape
    hf = b_forget.shape[0]
    wf = hf * HEAD_DIM
    wm = (w_in.shape[1] - 3 * wf - hf - 2 * d) // 3
    hm = wm // HEAD_DIM
    mp = nb_p * t_p
    n_pool, page = cache_fk.shape[0], cache_fk.shape[1]
    past_len = page_table.shape[1] * page
    assert t_p % MOBA_BLOCK == 0 and past_len % MOBA_BLOCK == 0 and MOBA_BLOCK % page == 0 and t_s <= MOBA_BLOCK
    assert t_p // MOBA_BLOCK <= LANES and past_len // MOBA_BLOCK <= LANES and hf <= LANES

    w_head = w_in[:, :3 * wf].astype(BF16)
    w_flin = jnp.pad(w_in[:, 3 * wf:3 * wf + hf], ((0, 0), (0, LANES - hf))).astype(BF16)
    w_tail = w_in[:, 3 * wf + hf:].astype(BF16)
    b_f = jnp.pad(b_forget, (0, LANES - hf))[None, :]
    x_bf = x_all.astype(BF16)
    pos = jnp.concatenate([jnp.tile(jnp.arange(t_p), nb_p), past_len + jnp.tile(jnp.arange(t_s), nb_s)])
    tables = _rope_tables(pos)

    tn = _divisor_tile(wf, 512, LANES)
    (q_f,) = _proj(x_bf, w_head, 0, wf, tn=tn, emit_bf16=True, name="proj_fox_q")
    k_f, k_f16 = _proj(x_bf, w_head, wf, wf, tn=tn, emit_f32=True, emit_bf16=True, name="proj_fox_k")
    v_f, v_f16 = _proj(x_bf, w_head, 2 * wf, wf, tn=tn, emit_f32=True, emit_bf16=True, name="proj_fox_v")
    (logf_pad,) = _proj(x_bf, w_flin, 0, LANES, tn=LANES, mode="logf", bias=b_f, emit_f32=True, name="proj_fox_logf")
    tn = _divisor_tile(wm, 512, LANES)
    (q_m,) = _proj(x_bf, w_tail, 0, wm, tn=tn, mode="rope", tables=tables, emit_bf16=True, name="proj_moba_q")
    k_m, k_m16 = _proj(x_bf, w_tail, wm, wm, tn=tn, mode="rope", tables=tables, emit_f32=True, emit_bf16=True,
                       name="proj_moba_k")
    v_m, v_m16 = _proj(x_bf, w_tail, 2 * wm, wm, tn=tn, emit_f32=True, emit_bf16=True, name="proj_moba_v")
    (gates,) = _proj(x_bf, w_tail, 3 * wm, 2 * d, tn=_divisor_tile(math.gcd(3 * wm, d), 512, LANES), emit_f32=True,
                     name="proj_gates")

    cum_p = _cumsum_rows(logf_pad[:mp], nb_p, t_p)
    cum_t = cum_p[:, :hf].reshape(nb_p, t_p, hf).transpose(0, 2, 1)
    o_f_p = _fox_prompt(q_f, k_f16, v_f16, cum_p, cum_t, nb_p, t_p, hf)

    nblk = t_p // MOBA_BLOCK
    kmean = _block_means(k_m, nb_p * nblk).reshape(nb_p, nblk, wm)
    kmean = jnp.pad(kmean, ((0, 0), (0, LANES - nblk), (0, 0)))
    o_m_p = _moba_prompt(q_m, k_m16, v_m16, kmean, nb_p, t_p, hm)

    def q_rows(q, h):
        return q[mp:].reshape(nb_s, t_s, h, HEAD_DIM).transpose(0, 2, 1, 3).reshape(nb_s, h * t_s, HEAD_DIM)

    def kv_rows(a, h):
        return a[mp:].reshape(nb_s, t_s * h, HEAD_DIM)

    def out_rows(o, h):
        return o.reshape(nb_s, h, t_s, HEAD_DIM).transpose(0, 2, 1, 3).reshape(nb_s * t_s, h * HEAD_DIM).astype(BF16)

    t_pad = -(-t_s // LANES) * LANES
    logf_s = jnp.pad(logf_pad[mp:].reshape(nb_s, t_s, LANES), ((0, 0), (0, t_pad - t_s), (0, 0)))
    cum_s = _cumsum_rows(logf_s.reshape(nb_s * t_pad, LANES), nb_s, t_pad).reshape(nb_s, t_pad, LANES)[:, :t_s, :hf]
    decay = _past_decay(cache_flog, page_table).reshape(nb_s, page_table.shape[1], 1, page * hf)
    cq_b = jnp.broadcast_to(cum_s.transpose(0, 2, 1).reshape(nb_s, hf * t_s, 1), (nb_s, hf * t_s, LANES))
    o_f_s = _fox_sample(page_table, q_rows(q_f, hf), cache_fk.reshape(n_pool, page * hf, HEAD_DIM),
                        cache_fv.reshape(n_pool, page * hf, HEAD_DIM), decay, cq_b,
                        kv_rows(k_f16, hf), kv_rows(v_f16, hf), cum_s.reshape(nb_s, 1, t_s * hf),
                        _head_masks(hf, t_s, page))
    o_m_s = _moba_sample(page_table, q_rows(q_m, hm), cache_mk.reshape(n_pool, page * hm, HEAD_DIM),
                         cache_mv.reshape(n_pool, page * hm, HEAD_DIM), kv_rows(k_m16, hm), kv_rows(v_m16, hm),
                         _head_masks(hm, t_s, page), hm, t_s)

    o_f = jnp.concatenate([o_f_p, out_rows(o_f_s, hf)], axis=0)
    o_m = jnp.concatenate([o_m_p, out_rows(o_m_s, hm)], axis=0)

    merged = _merge(o_f, o_m, w_br_fox.astype(BF16), w_br_moba.astype(BF16), gates, d)
    n_experts = w_router.shape[1]
    wr = jnp.pad(w_router, ((0, 0), (0, LANES - n_experts)))
    wr_hi = wr.astype(BF16)
    wr_lo = (wr - wr_hi.astype(F32)).astype(BF16)
    b_r = jnp.pad(b_router, (0, LANES - n_experts))[None, :]
    h, idx_pad, w_pad = _out_ln_route(merged, w_o.astype(BF16), x_all, ln1_g[None, :], ln1_b[None, :],
                                      wr_hi, wr_lo, b_r, alpha, n_experts)

    y = _moe_ln(h, idx_pad[:, :TOP_K], w_pad, w_gate, b_gate, w_up, b_up, w_down, b_down,
                ln2_g[None, :], ln2_b[None, :], alpha)

    def split_rows(a, h):
        return (a[:mp].reshape(nb_p, t_p, h, HEAD_DIM), a[mp:].reshape(nb_s, t_s, h, HEAD_DIM))

    logf = logf_pad[:, :hf]
    rows_p, rows_s = zip(split_rows(k_f, hf), split_rows(v_f, hf),
                         (logf[:mp].reshape(nb_p, t_p, hf), logf[mp:].reshape(nb_s, t_s, hf)),
                         split_rows(k_m, hm), split_rows(v_m, hm))
    return y, rows_p, rows_s


def _take_layer(a, l):
    return a.reshape(a.shape[1:]) if a.shape[0] == 1 else a[l]


def kernel(x_prompt, x_sample, cache_fox_k, cache_fox_v, cache_fox_logf, cache_moba_k, cache_moba_v, page_table, w_in, b_forget, w_br_fox, w_br_moba, w_o, ln1_g, ln1_b, w_router, b_router, w_gate, b_gate, w_up, b_up, w_down, b_down, ln2_g, ln2_b):
    depth = w_in.shape[0]
    alpha = (2.0 * depth) ** 0.25
    nb_p, t_p, d = x_prompt.shape
    nb_s, t_s, _ = x_sample.shape
    mp = nb_p * t_p
    x_all = jnp.concatenate([x_prompt.reshape(mp, d), x_sample.reshape(nb_s * t_s, d)], axis=0)
    rows_p, rows_s = [], []
    for l in range(depth):
        lw = tuple(_take_layer(w, l) for w in (w_in, b_forget, w_br_fox, w_br_moba, w_o, ln1_g, ln1_b, w_router, b_router,
                                               w_gate, b_gate, w_up, b_up, w_down, b_down, ln2_g, ln2_b))
        caches = tuple(_take_layer(c, l) for c in (cache_fox_k, cache_fox_v, cache_fox_logf, cache_moba_k, cache_moba_v))
        x_all, r_p, r_s = _layer(x_all, (nb_p, t_p, nb_s, t_s), caches, page_table, lw, alpha)
        rows_p.append(r_p)
        rows_s.append(r_s)

    def stack(rows, i):
        return jnp.stack([r[i] for r in rows])

    return (x_all[:mp].reshape(nb_p, t_p, d), x_all[mp:].reshape(nb_s, t_s, d),
            stack(rows_p, 0), stack(rows_p, 1), stack(rows_p, 2), stack(rows_p, 3), stack(rows_p, 4),
            stack(rows_s, 0), stack(rows_s, 1), stack(rows_s, 2), stack(rows_s, 3), stack(rows_s, 4))
```

```python
import functools
import math

import jax
import jax.numpy as jnp
from jax import lax
from jax.experimental import pallas as pl
from jax.experimental.pallas import tpu as pltpu

F32 = jnp.float32
BF16 = jnp.bfloat16
I32 = jnp.int32

HEAD_DIM = 128
LANES = 128
MOBA_BLOCK = 256
MOBA_TOPK = 3
ROPE_THETA = 10000.0
TOP_K = 4
SWIGLU_LIMIT = 7.0
SWIGLU_ALPHA = 1.702
LN_EPS = 1e-5
EXPERT_TILE = 256
NEG = -1e30
VMEM_LIMIT = 56 * 1024 * 1024

_NT = (((1,), (1,)), ((), ()))


def _divisor_tile(n, target, mult):
    best = None
    for t in range(mult, min(n, target) + 1, mult):
        if n % t == 0:
            best = t
    return n if best is None else best


def _params(sem):
    return pltpu.CompilerParams(dimension_semantics=sem, vmem_limit_bytes=VMEM_LIMIT)


def _split3(x):
    hi = x.astype(BF16)
    r1 = x - hi.astype(F32)
    mid = r1.astype(BF16)
    lo = (r1 - mid.astype(F32)).astype(BF16)
    return hi, mid, lo


def _layer_norm(u, g, b):
    mu = jnp.mean(u, axis=-1, keepdims=True)
    d = u - mu
    var = jnp.mean(d * d, axis=-1, keepdims=True)
    return d * lax.rsqrt(var + LN_EPS) * g + b


def _topk_lanes(vals, k):
    lane = lax.broadcasted_iota(I32, vals.shape, 1)
    out = []
    for _ in range(k):
        mx = jnp.max(vals, axis=1, keepdims=True)
        first = jnp.min(jnp.where(vals == mx, lane, vals.shape[1]), axis=1, keepdims=True)
        out.append((mx, first))
        vals = jnp.where(lane == first, -jnp.inf, vals)
    return out


def _proj_body(*refs, mode, emit_f32, emit_bf16):
    x_ref, w_ref = refs[0], refs[1]
    rest = list(refs[2:])
    acc = jnp.dot(x_ref[...], w_ref[...], preferred_element_type=F32)
    cos = sin = None
    if mode == "rope":
        cos, sin = rest.pop(0)[...], rest.pop(0)[...]
    elif mode == "logf":
        acc = jax.nn.log_sigmoid(acc + rest.pop(0)[...])
    o32 = rest.pop(0) if emit_f32 else None
    o16 = rest.pop(0) if emit_bf16 else None
    for c in range(acc.shape[1] // HEAD_DIM):
        sl = slice(c * HEAD_DIM, (c + 1) * HEAD_DIM)
        blk = acc[:, sl]
        if mode == "rope":
            blk = blk * cos + pltpu.roll(blk, HEAD_DIM // 2, 1) * sin
        if emit_f32:
            o32[:, sl] = blk
        if emit_bf16:
            o16[:, sl] = blk.astype(BF16)


def _proj(x, w, n0, n, *, tn, mode="plain", tables=None, bias=None, emit_f32=False, emit_bf16=False, name):
    m, k = x.shape
    tm = _divisor_tile(m, 1056, 16)
    assert n0 % tn == 0 and n % tn == 0
    j0 = n0 // tn
    in_specs = [pl.BlockSpec((tm, k), lambda i, j: (i, 0)), pl.BlockSpec((k, tn), lambda i, j: (0, j + j0))]
    args = [x, w]
    if mode == "rope":
        in_specs += [pl.BlockSpec((tm, HEAD_DIM), lambda i, j: (i, 0))] * 2
        args += list(tables)
    elif mode == "logf":
        in_specs.append(pl.BlockSpec((1, tn), lambda i, j: (0, j)))
        args.append(bias)
    out_shape, out_specs = [], []
    for flag, dt in ((emit_f32, F32), (emit_bf16, BF16)):
        if flag:
            out_shape.append(jax.ShapeDtypeStruct((m, n), dt))
            out_specs.append(pl.BlockSpec((tm, tn), lambda i, j: (i, j)))
    return pl.pallas_call(
        functools.partial(_proj_body, mode=mode, emit_f32=emit_f32, emit_bf16=emit_bf16),
        out_shape=out_shape, grid=(m // tm, n // tn), in_specs=in_specs, out_specs=out_specs,
        compiler_params=_params(("parallel", "arbitrary")), name=name)(*args)


def _cumsum_body(x_ref, o_ref, carry, *, tc):
    @pl.when(pl.program_id(1) == 0)
    def _():
        carry[...] = jnp.zeros_like(carry)
    x = x_ref[...]
    row = lax.broadcasted_iota(I32, (tc, tc), 0)
    col = lax.broadcasted_iota(I32, (tc, tc), 1)
    tri = (row >= col).astype(BF16)
    y = carry[...]
    for piece in _split3(x):
        y = y + jnp.dot(tri, piece, preferred_element_type=F32)
    o_ref[...] = y
    carry[...] = carry[...] + jnp.sum(x, axis=0, keepdims=True)


def _cumsum_rows(x, nb, t):
    tc = _divisor_tile(t, 512, 8)
    nc = t // tc
    return pl.pallas_call(
        functools.partial(_cumsum_body, tc=tc),
        out_shape=jax.ShapeDtypeStruct(x.shape, F32), grid=(nb, nc),
        in_specs=[pl.BlockSpec((tc, LANES), lambda b, c: (b * nc + c, 0))],
        out_specs=pl.BlockSpec((tc, LANES), lambda b, c: (b * nc + c, 0)),
        scratch_shapes=[pltpu.VMEM((1, LANES), F32)],
        compiler_params=_params(("parallel", "arbitrary")), name="cumsum_rows")(x)


def _decay_body(pt_ref, *refs, page, g):
    x_refs, o_ref, carry = refs[:g], refs[g], refs[g + 1]

    @pl.when(pl.program_id(1) == 0)
    def _():
        carry[...] = jnp.zeros_like(carry)
    row = lax.broadcasted_iota(I32, (page, page), 0)
    col = lax.broadcasted_iota(I32, (page, page), 1)
    tri = (col > row).astype(BF16)
    c = carry[...]
    for p in range(g):
        x = x_refs[p][0]
        y = jnp.broadcast_to(c, x.shape)
        for piece in _split3(x):
            y = y + jnp.dot(tri, piece, preferred_element_type=F32)
        o_ref[0, g - 1 - p] = y
        c = c + jnp.sum(x, axis=0, keepdims=True)
    carry[...] = c


def _past_decay(cache_logf, page_table):
    _, page, h = cache_logf.shape
    nb, n_pages = page_table.shape
    g = _pages_per_step(n_pages, 16)
    n_steps = n_pages // g

    def page_map(p):
        return lambda b, j, pt: (pt[b, n_pages - 1 - (j * g + p)], 0, 0)

    grid_spec = pltpu.PrefetchScalarGridSpec(
        num_scalar_prefetch=1, grid=(nb, n_steps),
        in_specs=[pl.BlockSpec((1, page, h), page_map(p)) for p in range(g)],
        out_specs=pl.BlockSpec((1, g, page, h), lambda b, j, pt: (b, n_steps - 1 - j, 0, 0)),
        scratch_shapes=[pltpu.VMEM((1, h), F32)])
    return pl.pallas_call(
        functools.partial(_decay_body, page=page, g=g),
        out_shape=jax.ShapeDtypeStruct((nb, n_pages, page, h), F32), grid_spec=grid_spec,
        compiler_params=_params(("parallel", "arbitrary")), name="past_decay")(page_table, *([cache_logf] * g))


def _online_update(zs, vs, m_sc, l_sc, acc_sc):
    m_old = m_sc[...]
    m_new = functools.reduce(jnp.maximum, [jnp.max(z, axis=1, keepdims=True) for z in zs], m_old)
    alpha = jnp.exp(m_old - m_new)
    ps = [jnp.exp(z - m_new) for z in zs]
    l_sc[...] = alpha * l_sc[...] + sum(jnp.sum(p, axis=1, keepdims=True) for p in ps)
    acc_sc[...] = alpha * acc_sc[...] + sum(jnp.dot(p.astype(BF16), v, preferred_element_type=F32)
                                            for p, v in zip(ps, vs))
    m_sc[...] = m_new


def _init_softmax(m_sc, l_sc, acc_sc):
    m_sc[...] = jnp.full_like(m_sc, -jnp.inf)
    l_sc[...] = jnp.zeros_like(l_sc)
    acc_sc[...] = jnp.zeros_like(acc_sc)


def _fox_prompt_body(q_ref, k_ref, v_ref, cum_ref, cumt_ref, o_ref, m_sc, l_sc, acc_sc, *, scale, tq):
    h, qi = pl.program_id(1), pl.program_id(2)
    _init_softmax(m_sc, l_sc, acc_sc)
    q = q_ref[...]
    lane = lax.broadcasted_iota(I32, cum_ref.shape, 1)
    cq = jnp.sum(jnp.where(lane == h, cum_ref[...], 0.0), axis=1, keepdims=True)

    def logits(ki):
        start = pl.multiple_of(ki * tq, tq)
        s = lax.dot_general(q, k_ref[pl.ds(start, tq), :], _NT, preferred_element_type=F32)
        return (s * scale + cq) - cumt_ref[0, pl.ds(h, 1), pl.ds(start, tq)], v_ref[pl.ds(start, tq), :]

    def pair(j, carry):
        z0, v0 = logits(2 * j)
        z1, v1 = logits(2 * j + 1)
        _online_update([z0, z1], [v0, v1], m_sc, l_sc, acc_sc)
        return carry

    lax.fori_loop(0, qi // 2, pair, 0)

    @pl.when(qi % 2 == 1)
    def _():
        z, v = logits(qi - 1)
        _online_update([z], [v], m_sc, l_sc, acc_sc)

    z, v = logits(qi)
    row = lax.broadcasted_iota(I32, z.shape, 0)
    col = lax.broadcasted_iota(I32, z.shape, 1)
    _online_update([jnp.where(row >= col, z, NEG)], [v], m_sc, l_sc, acc_sc)
    o_ref[...] = (acc_sc[...] / l_sc[...]).astype(o_ref.dtype)


def _fox_prompt(q, k, v, cum, cum_t, nb, t, n_heads):
    tq = _divisor_tile(t, 512, LANES)
    nq = t // tq
    seq = lambda b, h, qi: (b, h)
    return pl.pallas_call(
        functools.partial(_fox_prompt_body, scale=HEAD_DIM ** -0.5, tq=tq),
        out_shape=jax.ShapeDtypeStruct((nb * t, n_heads * HEAD_DIM), BF16),
        grid=(nb, n_heads, nq),
        in_specs=[pl.BlockSpec((tq, HEAD_DIM), lambda b, h, qi: (b * nq + qi, h)),
                  pl.BlockSpec((t, HEAD_DIM), seq),
                  pl.BlockSpec((t, HEAD_DIM), seq),
                  pl.BlockSpec((tq, LANES), lambda b, h, qi: (b * nq + qi, 0)),
                  pl.BlockSpec((1, n_heads, t), lambda b, h, qi: (b, 0, 0))],
        out_specs=pl.BlockSpec((tq, HEAD_DIM), lambda b, h, qi: (b * nq + qi, h)),
        scratch_shapes=[pltpu.VMEM((tq, 1), F32), pltpu.VMEM((tq, 1), F32), pltpu.VMEM((tq, HEAD_DIM), F32)],
        compiler_params=_params(("parallel", "parallel", "arbitrary")), name="fox_prompt")(q, k, v, cum, cum_t)


def _block_mean_body(k_ref, o_ref, *, blk):
    o_ref[0] = jnp.sum(k_ref[...], axis=0, keepdims=True) * (1.0 / blk)


def _block_means(k, n_blocks):
    w = k.shape[1]
    return pl.pallas_call(
        functools.partial(_block_mean_body, blk=MOBA_BLOCK),
        out_shape=jax.ShapeDtypeStruct((n_blocks, 1, w), F32), grid=(n_blocks,),
        in_specs=[pl.BlockSpec((MOBA_BLOCK, w), lambda n: (n, 0))],
        out_specs=pl.BlockSpec((1, 1, w), lambda n: (n, 0, 0)),
        compiler_params=_params(("parallel",)), name="moba_block_means")(k)


def _select_blocks(gate, n_valid):
    lane = lax.broadcasted_iota(I32, gate.shape, 1)
    picks = _topk_lanes(jnp.where(lane < n_valid, gate, -jnp.inf), MOBA_TOPK)
    sel = jnp.zeros(gate.shape, F32)
    for mx, first in picks:
        sel = jnp.where((lane == first) & (mx > -jnp.inf), 1.0, sel)
    return sel


def _moba_prompt_body(q_ref, k_ref, v_ref, km_ref, o_ref, m_sc, l_sc, acc_sc, sel_sc, *, scale, tq):
    blk = MOBA_BLOCK
    bpt = tq // blk
    qi = pl.program_id(2)
    _init_softmax(m_sc, l_sc, acc_sc)
    q = q_ref[...]
    own = qi * bpt + lax.broadcasted_iota(I32, (tq, 1), 0) // blk
    gate = sum(lax.dot_general(q, piece, _NT, preferred_element_type=F32) for piece in _split3(km_ref[0]))
    sel_sc[...] = _select_blocks(gate, own)

    def block(n, first_row=0):
        start = pl.multiple_of(n * blk, blk)
        qr = q_ref[first_row:, :]
        s = lax.dot_general(qr, k_ref[pl.ds(start, blk), :], _NT, preferred_element_type=F32) * scale
        lane = lax.broadcasted_iota(I32, (tq - first_row, LANES), 1)
        picked = jnp.max(jnp.where(lane == n, sel_sc[first_row:, :], 0.0), axis=1, keepdims=True)
        return s, picked, v_ref[pl.ds(start, blk), :]

    def pair(j, carry):
        zs, vs = [], []
        for a in range(min(bpt, 2)):
            s, picked, v = block(min(bpt, 2) * j + a)
            zs.append(jnp.where(picked > 0.0, s, NEG))
            vs.append(v)
        _online_update(zs, vs, m_sc, l_sc, acc_sc)
        return carry

    lax.fori_loop(0, qi * bpt // min(bpt, 2), pair, 0)

    for a in range(bpt):
        rows = slice(a * blk, tq)
        s, picked, v = block(qi * bpt + a, a * blk)
        row = lax.broadcasted_iota(I32, s.shape, 0)
        col = lax.broadcasted_iota(I32, s.shape, 1)
        z = jnp.where(row < blk, jnp.where(row >= col, s, NEG), jnp.where(picked > 0.0, s, NEG))
        _online_update([z], [v], m_sc.at[rows], l_sc.at[rows], acc_sc.at[rows])
    o_ref[...] = (acc_sc[...] / l_sc[...]).astype(o_ref.dtype)


def _moba_prompt(q, k, v, kmean, nb, t, n_heads):
    tq = _divisor_tile(t, 2 * MOBA_BLOCK, MOBA_BLOCK)
    nq = t // tq
    seq = lambda b, h, qi: (b, h)
    return pl.pallas_call(
        functools.partial(_moba_prompt_body, scale=HEAD_DIM ** -0.5, tq=tq),
        out_shape=jax.ShapeDtypeStruct((nb * t, n_heads * HEAD_DIM), BF16),
        grid=(nb, n_heads, nq),
        in_specs=[pl.BlockSpec((tq, HEAD_DIM), lambda b, h, qi: (b * nq + qi, h)),
                  pl.BlockSpec((t, HEAD_DIM), seq),
                  pl.BlockSpec((t, HEAD_DIM), seq),
                  pl.BlockSpec((1, LANES, HEAD_DIM), lambda b, h, qi: (b, 0, h))],
        out_specs=pl.BlockSpec((tq, HEAD_DIM), lambda b, h, qi: (b * nq + qi, h)),
        scratch_shapes=[pltpu.VMEM((tq, 1), F32), pltpu.VMEM((tq, 1), F32), pltpu.VMEM((tq, HEAD_DIM), F32),
                        pltpu.VMEM((tq, LANES), F32)],
        compiler_params=_params(("parallel", "parallel", "arbitrary")), name="moba_prompt")(q, k, v, kmean)


def _head_masks(n_heads, tq, page):
    rows = jnp.arange(n_heads * tq)[:, None]
    cols = jnp.arange(page * n_heads)[None, :]
    past = jnp.where(rows // tq == cols % n_heads, 0.0, NEG).astype(F32)
    ncol = jnp.arange(tq * n_heads)[None, :]
    new = jnp.where((rows // tq == ncol % n_heads) & (ncol // n_heads <= rows % tq), 0.0, NEG).astype(F32)
    return past, new


def _fox_sample_body(pt_ref, q_ref, *refs, scale, n_steps, g):
    k_refs, v_refs = refs[:g], refs[g:2 * g]
    d_ref, cq_ref, pm_ref, kn_ref, vn_ref, dn_ref, nm_ref, o_ref, m_sc, l_sc, acc_sc, bias_sc = refs[2 * g:]
    j = pl.program_id(1)
    cq = cq_ref[0][:, :1]

    @pl.when(j == 0)
    def _():
        _init_softmax(m_sc, l_sc, acc_sc)
        bias_sc[...] = pm_ref[...] + cq

    @pl.when(j < n_steps)
    def _():
        q = q_ref[0]
        zs = [lax.dot_general(q, k_refs[p][0].astype(BF16), _NT, preferred_element_type=F32) * scale
              + bias_sc[...] + d_ref[0, p] for p in range(g)]
        _online_update(zs, [v_refs[p][0].astype(BF16) for p in range(g)], m_sc, l_sc, acc_sc)

    @pl.when(j == n_steps)
    def _():
        s = lax.dot_general(q_ref[0], kn_ref[0], _NT, preferred_element_type=F32)
        z = s * scale + nm_ref[...] + (cq - dn_ref[0])
        _online_update([z], [vn_ref[0]], m_sc, l_sc, acc_sc)
        o_ref[0] = acc_sc[...] / l_sc[...]


def _pages_per_step(n_pages, target):
    return max(c for c in range(1, target + 1) if n_pages % c == 0)


def _fox_sample(page_table, q_all, cache_k, cache_v, decay_flat, cq_b, k_new, v_new, cum_new_flat, masks):
    nb, r, _ = q_all.shape
    n_pages = page_table.shape[1]
    pw = cache_k.shape[1]
    nw = k_new.shape[1]
    pmask, nmask = masks
    g = _pages_per_step(n_pages, 4)
    n_steps = n_pages // g

    def page_map(p):
        return lambda b, j, pt: (pt[b, jnp.minimum(j, n_steps - 1) * g + p], 0, 0)

    per_b = lambda b, j, pt: (b, 0, 0)
    const = lambda b, j, pt: (0, 0)
    page_specs = [pl.BlockSpec((1, pw, HEAD_DIM), page_map(p)) for p in range(g)]
    grid_spec = pltpu.PrefetchScalarGridSpec(
        num_scalar_prefetch=1, grid=(nb, n_steps + 1),
        in_specs=[pl.BlockSpec((1, r, HEAD_DIM), per_b)] + page_specs + page_specs +
                 [pl.BlockSpec((1, g, 1, pw), lambda b, j, pt: (b, jnp.minimum(j, n_steps - 1), 0, 0)),
                  pl.BlockSpec((1, r, LANES), per_b),
                  pl.BlockSpec((r, pw), const),
                  pl.BlockSpec((1, nw, HEAD_DIM), per_b),
                  pl.BlockSpec((1, nw, HEAD_DIM), per_b),
                  pl.BlockSpec((1, 1, nw), per_b),
                  pl.BlockSpec((r, nw), const)],
        out_specs=pl.BlockSpec((1, r, HEAD_DIM), per_b),
        scratch_shapes=[pltpu.VMEM((r, 1), F32), pltpu.VMEM((r, 1), F32), pltpu.VMEM((r, HEAD_DIM), F32),
                        pltpu.VMEM((r, pw), F32)])
    return pl.pallas_call(
        functools.partial(_fox_sample_body, scale=HEAD_DIM ** -0.5, n_steps=n_steps, g=g),
        out_shape=jax.ShapeDtypeStruct((nb, r, HEAD_DIM), F32), grid_spec=grid_spec,
        compiler_params=_params(("parallel", "arbitrary")), name="fox_sample")(
            page_table, q_all, *([cache_k] * g), *([cache_v] * g), decay_flat, cq_b, pmask, k_new, v_new,
            cum_new_flat, nmask)


def _moba_sample_body(pt_ref, q_ref, *refs, scale, n_past, ppb, bps, n_heads, tq):
    npg = ppb * bps
    k_refs, v_refs = refs[:npg], refs[npg:2 * npg]
    pm_ref, kn_ref, vn_ref, nm_ref, o_ref, accs, gate_sc, mx_sc, l_sc = refs[2 * npg:]
    j = pl.program_id(1)
    q = q_ref[0]
    r = q.shape[0]
    lane = lax.broadcasted_iota(I32, (r, LANES), 1)

    def block_softmax(ks, vs, mask):
        zs = [lax.dot_general(q, kb, _NT, preferred_element_type=F32) * scale + mask for kb in ks]
        m = functools.reduce(jnp.maximum, [jnp.max(z, axis=1, keepdims=True) for z in zs])
        ps = [jnp.exp(z - m) for z in zs]
        l = sum(jnp.sum(p, axis=1, keepdims=True) for p in ps)
        acc = sum(jnp.dot(p.astype(BF16), vb, preferred_element_type=F32) for p, vb in zip(ps, vs))
        return m, l, acc

    @pl.when(j == 0)
    def _():
        gate_sc[...] = jnp.zeros_like(gate_sc)
        mx_sc[...] = jnp.zeros_like(mx_sc)
        l_sc[...] = jnp.zeros_like(l_sc)

    @pl.when(j < n_past // bps)
    def _():
        for a in range(bps):
            n = j * bps + a
            kf = [kr[0] for kr in k_refs[a * ppb:(a + 1) * ppb]]
            vb = [vr[0].astype(BF16) for vr in v_refs[a * ppb:(a + 1) * ppb]]
            m, l, acc = block_softmax([x.astype(BF16) for x in kf], vb, pm_ref[...])
            accs[n] = acc
            tok = kf[0].shape[0] // n_heads
            ksum = sum(jnp.sum(x.reshape(tok, n_heads, HEAD_DIM), axis=0) for x in kf)
            kmean = ksum * (1.0 / (tok * ppb))
            kexp = jnp.broadcast_to(kmean[:, None, :], (n_heads, tq, HEAD_DIM)).reshape(r, HEAD_DIM)
            gate = jnp.sum(q.astype(F32) * kexp, axis=1, keepdims=True)
            gate_sc[...] = jnp.where(lane == n, gate, gate_sc[...])
            mx_sc[...] = jnp.where(lane == n, m, mx_sc[...])
            l_sc[...] = jnp.where(lane == n, l, l_sc[...])

    @pl.when(j == n_past // bps)
    def _():
        m_own, l_own, acc_own = block_softmax([kn_ref[0]], [vn_ref[0]], nm_ref[...])
        sel = _select_blocks(gate_sc[...], n_past)
        m_all = jnp.maximum(jnp.max(jnp.where(sel > 0.0, mx_sc[...], -jnp.inf), axis=1, keepdims=True), m_own)
        wn = jnp.where(sel > 0.0, jnp.exp(mx_sc[...] - m_all), 0.0)
        w_own = jnp.exp(m_own - m_all)
        denom = jnp.sum(wn * l_sc[...], axis=1, keepdims=True) + w_own * l_own

        def add_block(i, num):
            wi = jnp.sum(jnp.where(lane == i, wn, 0.0), axis=1, keepdims=True)
            return num + wi * accs[i]

        num = lax.fori_loop(0, n_past, add_block, w_own * acc_own)
        o_ref[0] = num / denom


def _moba_sample(page_table, q_all, cache_k, cache_v, k_new, v_new, masks, n_heads, tq):
    nb, r, _ = q_all.shape
    n_pages = page_table.shape[1]
    pw = cache_k.shape[1]
    nw = k_new.shape[1]
    page = pw // n_heads
    ppb = MOBA_BLOCK // page
    n_past = n_pages // ppb
    pmask, nmask = masks
    bps = _pages_per_step(n_past, 2)
    npg = ppb * bps
    n_steps = n_past // bps

    def page_map(p):
        return lambda b, j, pt: (pt[b, jnp.minimum(j, n_steps - 1) * npg + p], 0, 0)

    per_b = lambda b, j, pt: (b, 0, 0)
    const = lambda b, j, pt: (0, 0)
    page_specs = [pl.BlockSpec((1, pw, HEAD_DIM), page_map(p)) for p in range(npg)]
    grid_spec = pltpu.PrefetchScalarGridSpec(
        num_scalar_prefetch=1, grid=(nb, n_steps + 1),
        in_specs=[pl.BlockSpec((1, r, HEAD_DIM), per_b)] + page_specs + page_specs +
                 [pl.BlockSpec((r, pw), const),
                  pl.BlockSpec((1, nw, HEAD_DIM), per_b),
                  pl.BlockSpec((1, nw, HEAD_DIM), per_b),
                  pl.BlockSpec((r, nw), const)],
        out_specs=pl.BlockSpec((1, r, HEAD_DIM), per_b),
        scratch_shapes=[pltpu.VMEM((n_past, r, HEAD_DIM), F32), pltpu.VMEM((r, LANES), F32),
                        pltpu.VMEM((r, LANES), F32), pltpu.VMEM((r, LANES), F32)])
    return pl.pallas_call(
        functools.partial(_moba_sample_body, scale=HEAD_DIM ** -0.5, n_past=n_past, ppb=ppb, bps=bps,
                          n_heads=n_heads, tq=tq),
        out_shape=jax.ShapeDtypeStruct((nb, r, HEAD_DIM), F32), grid_spec=grid_spec,
        compiler_params=_params(("parallel", "arbitrary")), name="moba_sample")(
            page_table, q_all, *([cache_k] * npg), *([cache_v] * npg), pmask, k_new, v_new, nmask)


def _merge_body(of_ref, om_ref, wf_ref, wm_ref, gf_ref, gm_ref, o_ref):
    br_f = jnp.dot(of_ref[...], wf_ref[...], preferred_element_type=F32)
    br_m = jnp.dot(om_ref[...], wm_ref[...], preferred_element_type=F32)
    o_ref[...] = (jax.nn.sigmoid(gf_ref[...]) * br_f + jax.nn.sigmoid(gm_ref[...]) * br_m).astype(o_ref.dtype)


def _merge(o_f, o_m, w_f, w_m, gates, d):
    m = o_f.shape[0]
    tm = _divisor_tile(m, 1056, 16)
    tn = _divisor_tile(d, 512, LANES)
    nj = d // tn
    return pl.pallas_call(
        _merge_body, out_shape=jax.ShapeDtypeStruct((m, d), BF16), grid=(m // tm, nj),
        in_specs=[pl.BlockSpec((tm, o_f.shape[1]), lambda i, j: (i, 0)),
                  pl.BlockSpec((tm, o_m.shape[1]), lambda i, j: (i, 0)),
                  pl.BlockSpec((w_f.shape[0], tn), lambda i, j: (0, j)),
                  pl.BlockSpec((w_m.shape[0], tn), lambda i, j: (0, j)),
                  pl.BlockSpec((tm, tn), lambda i, j: (i, j)),
                  pl.BlockSpec((tm, tn), lambda i, j: (i, j + nj))],
        out_specs=pl.BlockSpec((tm, tn), lambda i, j: (i, j)),
        compiler_params=_params(("parallel", "arbitrary")), name="branch_merge")(o_f, o_m, w_f, w_m, gates, gates)


def _out_ln_body(a_ref, w_ref, x_ref, g_ref, b_ref, wrh_ref, wrl_ref, br_ref, h_ref, idx_ref, wgt_ref, acc,
                 *, alpha, n_experts):
    k = pl.program_id(1)

    @pl.when(k == 0)
    def _():
        acc[...] = jnp.zeros_like(acc)

    acc[...] += jnp.dot(a_ref[...], w_ref[...], preferred_element_type=F32)

    @pl.when(k == pl.num_programs(1) - 1)
    def _():
        h = _layer_norm(alpha * x_ref[...] + acc[...], g_ref[...], b_ref[...])
        h_ref[...] = h
        h_hi = h.astype(BF16)
        h_lo = (h - h_hi.astype(F32)).astype(BF16)
        logits = (jnp.dot(h_hi, wrh_ref[...], preferred_element_type=F32)
                  + jnp.dot(h_hi, wrl_ref[...], preferred_element_type=F32)
                  + jnp.dot(h_lo, wrh_ref[...], preferred_element_type=F32)) + br_ref[...]
        lane = lax.broadcasted_iota(I32, logits.shape, 1)
        picks = _topk_lanes(jnp.where(lane < n_experts, logits, -jnp.inf), TOP_K)
        idx = jnp.zeros(logits.shape, I32)
        e = jnp.zeros(logits.shape, F32)
        for r, (mx, first) in enumerate(picks):
            idx = jnp.where(lane == r, first, idx)
            e = jnp.where(lane == r, jnp.exp(mx - picks[0][0]), e)
        idx_ref[...] = idx
        wgt_ref[...] = e / jnp.sum(e, axis=1, keepdims=True)


def _out_ln_route(merged, w_o, x, ln_g, ln_b, wr_hi, wr_lo, b_r, alpha, n_experts):
    m, d = x.shape
    tm = _divisor_tile(m, 264, 8)
    tk = _divisor_tile(d, 1024, LANES)
    row = lambda i, k: (i, 0)
    const = lambda i, k: (0, 0)
    return pl.pallas_call(
        functools.partial(_out_ln_body, alpha=alpha, n_experts=n_experts),
        out_shape=[jax.ShapeDtypeStruct((m, d), F32), jax.ShapeDtypeStruct((m, LANES), I32),
                   jax.ShapeDtypeStruct((m, LANES), F32)],
        grid=(m // tm, d // tk),
        in_specs=[pl.BlockSpec((tm, tk), lambda i, k: (i, k)), pl.BlockSpec((tk, d), lambda i, k: (k, 0)),
                  pl.BlockSpec((tm, d), row), pl.BlockSpec((1, d), const), pl.BlockSpec((1, d), const),
                  pl.BlockSpec((d, LANES), const), pl.BlockSpec((d, LANES), const), pl.BlockSpec((1, LANES), const)],
        out_specs=[pl.BlockSpec((tm, d), row), pl.BlockSpec((tm, LANES), row), pl.BlockSpec((tm, LANES), row)],
        scratch_shapes=[pltpu.VMEM((tm, d), F32)],
        compiler_params=_params(("parallel", "arbitrary")), name="out_proj_ln_router")(
            merged, w_o, x, ln_g, ln_b, wr_hi, wr_lo, b_r)


def _gather_rows_body(tok_ref, nu_ref, h_hbm, o_ref, buf, sem, *, tm):
    i = pl.program_id(0)
    n_used = nu_ref[0]

    def fetch(tile):
        slot = tile % 2

        def issue(r, c):
            pltpu.make_async_copy(h_hbm.at[pl.ds(tok_ref[tile * tm + r], 1)], buf.at[slot, pl.ds(r, 1)],
                                  sem.at[slot]).start()
            return c

        lax.fori_loop(0, tm, issue, 0)

    @pl.when((i == 0) & (n_used > 0))
    def _():
        fetch(0)

    @pl.when(i + 1 < n_used)
    def _():
        fetch(i + 1)

    @pl.when(i < n_used)
    def _():
        slot = i % 2
        pltpu.make_async_copy(h_hbm.at[pl.ds(0, tm)], buf.at[slot], sem.at[slot]).wait()
        o_ref[...] = buf[slot].astype(o_ref.dtype)

    @pl.when(i >= n_used)
    def _():
        o_ref[...] = jnp.zeros_like(o_ref)


def _gather_rows(h, buf_tok, n_used, tm):
    n_rows = buf_tok.shape[0]
    d = h.shape[1]
    grid_spec = pltpu.PrefetchScalarGridSpec(
        num_scalar_prefetch=2, grid=(n_rows // tm,),
        in_specs=[pl.BlockSpec(memory_space=pl.ANY)],
        out_specs=pl.BlockSpec((tm, d), lambda i, tok, nu: (i, 0)),
        scratch_shapes=[pltpu.VMEM((2, tm, d), F32), pltpu.SemaphoreType.DMA((2,))])
    return pl.pallas_call(
        functools.partial(_gather_rows_body, tm=tm),
        out_shape=jax.ShapeDtypeStruct((n_rows, d), BF16), grid_spec=grid_spec,
        compiler_params=_params(("arbitrary",)), name="moe_gather_rows")(buf_tok, n_used, h)


def _expert_changed(te_ref, i):
    return (i == 0) | (te_ref[i] != te_ref[jnp.maximum(i - 1, 0)])


def _gate_up_body(te_ref, nu_ref, x_ref, wg_ref, wu_ref, bg_ref, bu_ref, o_ref, wg_bf, wu_bf):
    i = pl.program_id(1)
    used = i < nu_ref[0]

    @pl.when(used & _expert_changed(te_ref, i))
    def _():
        wg_bf[...] = wg_ref[0].astype(BF16)
        wu_bf[...] = wu_ref[0].astype(BF16)

    @pl.when(used)
    def _():
        x = x_ref[...]
        gate = jnp.dot(x, wg_bf[...], preferred_element_type=F32) + bg_ref[0]
        up = jnp.dot(x, wu_bf[...], preferred_element_type=F32) + bu_ref[0]
        gate = jnp.minimum(gate, SWIGLU_LIMIT)
        up = jnp.clip(up, -SWIGLU_LIMIT, SWIGLU_LIMIT)
        glu = gate * jax.nn.sigmoid(gate * SWIGLU_ALPHA)
        o_ref[...] = ((up + 1.0) * glu).astype(o_ref.dtype)

    @pl.when(jnp.logical_not(used))
    def _():
        o_ref[...] = jnp.zeros_like(o_ref)


def _gate_up(xs, tile_e, n_used, w_gate, b_gate, w_up, b_up, tm):
    n_rows, d = xs.shape
    f = w_gate.shape[2]
    tf = _divisor_tile(f, 512, LANES)
    wmap = lambda j, i, te, nu: (te[i], 0, j)
    grid_spec = pltpu.PrefetchScalarGridSpec(
        num_scalar_prefetch=2, grid=(f // tf, n_rows // tm),
        in_specs=[pl.BlockSpec((tm, d), lambda j, i, te, nu: (i, 0)),
                  pl.BlockSpec((1, d, tf), wmap), pl.BlockSpec((1, d, tf), wmap),
                  pl.BlockSpec((1, 1, tf), wmap), pl.BlockSpec((1, 1, tf), wmap)],
        out_specs=pl.BlockSpec((tm, tf), lambda j, i, te, nu: (i, j)),
        scratch_shapes=[pltpu.VMEM((d, tf), BF16), pltpu.VMEM((d, tf), BF16)])
    return pl.pallas_call(
        _gate_up_body, out_shape=jax.ShapeDtypeStruct((n_rows, f), BF16), grid_spec=grid_spec,
        compiler_params=_params(("arbitrary", "arbitrary")), name="moe_gate_up")(
            tile_e, n_used, xs, w_gate, w_up, b_gate, b_up)


def _down_body(te_ref, nu_ref, a_ref, w_ref, b_ref, o_ref, w_bf):
    i = pl.program_id(1)
    used = i < nu_ref[0]

    @pl.when(used & _expert_changed(te_ref, i))
    def _():
        w_bf[...] = w_ref[0].astype(BF16)

    @pl.when(used)
    def _():
        o_ref[...] = jnp.dot(a_ref[...], w_bf[...], preferred_element_type=F32) + b_ref[0]

    @pl.when(jnp.logical_not(used))
    def _():
        o_ref[...] = jnp.zeros_like(o_ref)


def _down(act, tile_e, n_used, w_down, b_down, tm):
    n_rows, f = act.shape
    d = w_down.shape[2]
    tn = _divisor_tile(d, 1024, LANES)
    wmap = lambda j, i, te, nu: (te[i], 0, j)
    grid_spec = pltpu.PrefetchScalarGridSpec(
        num_scalar_prefetch=2, grid=(d // tn, n_rows // tm),
        in_specs=[pl.BlockSpec((tm, f), lambda j, i, te, nu: (i, 0)),
                  pl.BlockSpec((1, f, tn), wmap), pl.BlockSpec((1, 1, tn), wmap)],
        out_specs=pl.BlockSpec((tm, tn), lambda j, i, te, nu: (i, j)),
        scratch_shapes=[pltpu.VMEM((f, tn), BF16)])
    return pl.pallas_call(
        _down_body, out_shape=jax.ShapeDtypeStruct((n_rows, d), F32), grid_spec=grid_spec,
        compiler_params=_params(("arbitrary", "arbitrary")), name="moe_down")(tile_e, n_used, act, w_down, b_down)


def _combine_ln_body(pos_ref, y_hbm, h_ref, w_ref, g_ref, b_ref, o_ref, buf, sem, *, tc, alpha):
    i = pl.program_id(0)

    def fetch(tile):
        slot = tile % 2
        base = tile * tc * TOP_K

        def issue(r, c):
            for k in range(TOP_K):
                pltpu.make_async_copy(y_hbm.at[pl.ds(pos_ref[base + r * TOP_K + k], 1)],
                                      buf.at[slot, k, pl.ds(r, 1)], sem.at[slot]).start()
            return c

        lax.fori_loop(0, tc, issue, 0)

    @pl.when(i == 0)
    def _():
        fetch(0)

    @pl.when(i + 1 < pl.num_programs(0))
    def _():
        fetch(i + 1)

    slot = i % 2
    for k in range(TOP_K):
        pltpu.make_async_copy(y_hbm.at[pl.ds(0, tc)], buf.at[slot, k], sem.at[slot]).wait()
    w = w_ref[...]
    f = sum(w[:, k:k + 1] * buf[slot, k] for k in range(TOP_K))
    o_ref[...] = _layer_norm(alpha * h_ref[...] + f, g_ref[...], b_ref[...])


def _combine_ln(y_rows, pos, h, w_pad, ln_g, ln_b, alpha):
    m, d = h.shape
    tc = _divisor_tile(m, 128, 8)
    row = lambda i, p: (i, 0)
    const = lambda i, p: (0, 0)
    grid_spec = pltpu.PrefetchScalarGridSpec(
        num_scalar_prefetch=1, grid=(m // tc,),
        in_specs=[pl.BlockSpec(memory_space=pl.ANY), pl.BlockSpec((tc, d), row), pl.BlockSpec((tc, LANES), row),
                  pl.BlockSpec((1, d), const), pl.BlockSpec((1, d), const)],
        out_specs=pl.BlockSpec((tc, d), row),
        scratch_shapes=[pltpu.VMEM((2, TOP_K, tc, d), F32), pltpu.SemaphoreType.DMA((2,))])
    return pl.pallas_call(
        functools.partial(_combine_ln_body, tc=tc, alpha=alpha),
        out_shape=jax.ShapeDtypeStruct((m, d), F32), grid_spec=grid_spec,
        compiler_params=_params(("arbitrary",)), name="moe_combine_ln")(pos, y_rows, h, w_pad, ln_g, ln_b)


def _moe_ln(h, top_idx, w_pad, w_gate, b_gate, w_up, b_up, w_down, b_down, ln_g, ln_b, alpha):
    m, d = h.shape
    n_experts = w_gate.shape[0]
    tm = EXPERT_TILE
    n_assign = m * TOP_K
    flat_e = top_idx.reshape(-1)
    order = jnp.argsort(flat_e)
    sorted_e = flat_e[order]
    counts = jnp.bincount(flat_e, length=n_experts)
    padded = (counts + tm - 1) // tm * tm
    pend = jnp.cumsum(padded)
    dest = (pend - padded)[sorted_e] + jnp.arange(n_assign) - (jnp.cumsum(counts) - counts)[sorted_e]
    n_tiles = -(-(n_assign + n_experts * (tm - 1)) // tm)
    buf_tok = jnp.zeros((n_tiles * tm,), I32).at[dest].set((order // TOP_K).astype(I32))
    tile_e = jnp.minimum(jnp.searchsorted(pend, jnp.arange(n_tiles) * tm, side="right"), n_experts - 1).astype(I32)
    n_used = (pend[-1] // tm).astype(I32).reshape(1)
    pos = jnp.zeros((n_assign,), I32).at[order].set(dest.astype(I32))

    xs = _gather_rows(h, buf_tok, n_used, tm)
    act = _gate_up(xs, tile_e, n_used, w_gate, b_gate[:, None, :], w_up, b_up[:, None, :], tm)
    y_rows = _down(act, tile_e, n_used, w_down, b_down[:, None, :], tm)
    return _combine_ln(y_rows, pos, h, w_pad, ln_g, ln_b, alpha)


def _rope_tables(pos):
    half = HEAD_DIM // 2
    inv = ROPE_THETA ** (-jnp.arange(half, dtype=F32) / half)
    ang = pos.astype(F32)[:, None] * inv[None, :]
    cos, sin = jnp.cos(ang), jnp.sin(ang)
    return jnp.concatenate([cos, cos], axis=1), jnp.concatenate([-sin, sin], axis=1)


def _layer(x_all, dims, caches, page_table, lw, alpha):
    (w_in, b_forget, w_br_fox, w_br_moba, w_o, ln1_g, ln1_b, w_router, b_router,
     w_gate, b_gate, w_up, b_up, w_down, b_down, ln2_g, ln2_b) = lw
    nb_p, t_p, nb_s, t_s = dims
    cache_fk, cache_fv, cache_flog, cache_mk, cache_mv = caches
    m, d = x_all.shape
    hf = b_forget.shape[0]
    wf = hf * HEAD_DIM
    wm = (w_in.shape[1] - 3 * wf - hf - 2 * d) // 3
    hm = wm // HEAD_DIM
    mp = nb_p * t_p
    n_pool, page = cache_fk.shape[0], cache_fk.shape[1]
    past_len = page_table.shape[1] * page
    assert t_p % MOBA_BLOCK == 0 and past_len % MOBA_BLOCK == 0 and MOBA_BLOCK % page == 0 and t_s <= MOBA_BLOCK
    assert t_p // MOBA_BLOCK <= LANES and past_len // MOBA_BLOCK <= LANES and hf <= LANES

    w_head = w_in[:, :3 * wf].astype(BF16)
    w_flin = jnp.pad(w_in[:, 3 * wf:3 * wf + hf], ((0, 0), (0, LANES - hf))).astype(BF16)
    w_tail = w_in[:, 3 * wf + hf:].astype(BF16)
    b_f = jnp.pad(b_forget, (0, LANES - hf))[None, :]
    x_bf = x_all.astype(BF16)
    pos = jnp.concatenate([jnp.tile(jnp.arange(t_p), nb_p), past_len + jnp.tile(jnp.arange(t_s), nb_s)])
    tables = _rope_tables(pos)

    tn = _divisor_tile(wf, 512, LANES)
    (q_f,) = _proj(x_bf, w_head, 0, wf, tn=tn, emit_bf16=True, name="proj_fox_q")
    k_f, k_f16 = _proj(x_bf, w_head, wf, wf, tn=tn, emit_f32=True, emit_bf16=True, name="proj_fox_k")
    v_f, v_f16 = _proj(x_bf, w_head, 2 * wf, wf, tn=tn, emit_f32=True, emit_bf16=True, name="proj_fox_v")
    (logf_pad,) = _proj(x_bf, w_flin, 0, LANES, tn=LANES, mode="logf", bias=b_f, emit_f32=True, name="proj_fox_logf")
    tn = _divisor_tile(wm, 512, LANES)
    (q_m,) = _proj(x_bf, w_tail, 0, wm, tn=tn, mode="rope", tables=tables, emit_bf16=True, name="proj_moba_q")
    k_m, k_m16 = _proj(x_bf, w_tail, wm, wm, tn=tn, mode="rope", tables=tables, emit_f32=True, emit_bf16=True,
                       name="proj_moba_k")
    v_m, v_m16 = _proj(x_bf, w_tail, 2 * wm, wm, tn=tn, emit_f32=True, emit_bf16=True, name="proj_moba_v")
    (gates,) = _proj(x_bf, w_tail, 3 * wm, 2 * d, tn=_divisor_tile(math.gcd(3 * wm, d), 512, LANES), emit_f32=True,
                     name="proj_gates")

    cum_p = _cumsum_rows(logf_pad[:mp], nb_p, t_p)
    cum_t = cum_p[:, :hf].reshape(nb_p, t_p, hf).transpose(0, 2, 1)
    o_f_p = _fox_prompt(q_f, k_f16, v_f16, cum_p, cum_t, nb_p, t_p, hf)

    nblk = t_p // MOBA_BLOCK
    kmean = _block_means(k_m, nb_p * nblk).reshape(nb_p, nblk, wm)
    kmean = jnp.pad(kmean, ((0, 0), (0, LANES - nblk), (0, 0)))
    o_m_p = _moba_prompt(q_m, k_m16, v_m16, kmean, nb_p, t_p, hm)

    def q_rows(q, h):
        return q[mp:].reshape(nb_s, t_s, h, HEAD_DIM).transpose(0, 2, 1, 3).reshape(nb_s, h * t_s, HEAD_DIM)

    def kv_rows(a, h):
        return a[mp:].reshape(nb_s, t_s * h, HEAD_DIM)

    def out_rows(o, h):
        return o.reshape(nb_s, h, t_s, HEAD_DIM).transpose(0, 2, 1, 3).reshape(nb_s * t_s, h * HEAD_DIM).astype(BF16)

    t_pad = -(-t_s // LANES) * LANES
    logf_s = jnp.pad(logf_pad[mp:].reshape(nb_s, t_s, LANES), ((0, 0), (0, t_pad - t_s), (0, 0)))
    cum_s = _cumsum_rows(logf_s.reshape(nb_s * t_pad, LANES), nb_s, t_pad).reshape(nb_s, t_pad, LANES)[:, :t_s, :hf]
    decay = _past_decay(cache_flog, page_table).reshape(nb_s, page_table.shape[1], 1, page * hf)
    cq_b = jnp.broadcast_to(cum_s.transpose(0, 2, 1).reshape(nb_s, hf * t_s, 1), (nb_s, hf * t_s, LANES))
    o_f_s = _fox_sample(page_table, q_rows(q_f, hf), cache_fk.reshape(n_pool, page * hf, HEAD_DIM),
                        cache_fv.reshape(n_pool, page * hf, HEAD_DIM), decay, cq_b,
                        kv_rows(k_f16, hf), kv_rows(v_f16, hf), cum_s.reshape(nb_s, 1, t_s * hf),
                        _head_masks(hf, t_s, page))
    o_m_s = _moba_sample(page_table, q_rows(q_m, hm), cache_mk.reshape(n_pool, page * hm, HEAD_DIM),
                         cache_mv.reshape(n_pool, page * hm, HEAD_DIM), kv_rows(k_m16, hm), kv_rows(v_m16, hm),
                         _head_masks(hm, t_s, page), hm, t_s)

    o_f = jnp.concatenate([o_f_p, out_rows(o_f_s, hf)], axis=0)
    o_m = jnp.concatenate([o_m_p, out_rows(o_m_s, hm)], axis=0)

    merged = _merge(o_f, o_m, w_br_fox.astype(BF16), w_br_moba.astype(BF16), gates, d)
    n_experts = w_router.shape[1]
    wr = jnp.pad(w_router, ((0, 0), (0, LANES - n_experts)))
    wr_hi = wr.astype(BF16)
    wr_lo = (wr - wr_hi.astype(F32)).astype(BF16)
    b_r = jnp.pad(b_router, (0, LANES - n_experts))[None, :]
    h, idx_pad, w_pad = _out_ln_route(merged, w_o.astype(BF16), x_all, ln1_g[None, :], ln1_b[None, :],
                                      wr_hi, wr_lo, b_r, alpha, n_experts)

    y = _moe_ln(h, idx_pad[:, :TOP_K], w_pad, w_gate, b_gate, w_up, b_up, w_down, b_down,
                ln2_g[None, :], ln2_b[None, :], alpha)

    def split_rows(a, h):
        return (a[:mp].reshape(nb_p, t_p, h, HEAD_DIM), a[mp:].reshape(nb_s, t_s, h, HEAD_DIM))

    logf = logf_pad[:, :hf]
    rows_p, rows_s = zip(split_rows(k_f, hf), split_rows(v_f, hf),
                         (logf[:mp].reshape(nb_p, t_p, hf), logf[mp:].reshape(nb_s, t_s, hf)),
                         split_rows(k_m, hm), split_rows(v_m, hm))
    return y, rows_p, rows_s


def _take_layer(a, l):
    return a.reshape(a.shape[1:]) if a.shape[0] == 1 else a[l]


def kernel(x_prompt, x_sample, cache_fox_k, cache_fox_v, cache_fox_logf, cache_moba_k, cache_moba_v, page_table, w_in, b_forget, w_br_fox, w_br_moba, w_o, ln1_g, ln1_b, w_router, b_router, w_gate, b_gate, w_up, b_up, w_down, b_down, ln2_g, ln2_b):
    depth = w_in.shape[0]
    alpha = (2.0 * depth) ** 0.25
    nb_p, t_p, d = x_prompt.shape
    nb_s, t_s, _ = x_sample.shape
    mp = nb_p * t_p
    x_all = jnp.concatenate([x_prompt.reshape(mp, d), x_sample.reshape(nb_s * t_s, d)], axis=0)
    rows_p, rows_s = [], []
    for l in range(depth):
        lw = tuple(_take_layer(w, l) for w in (w_in, b_forget, w_br_fox, w_br_moba, w_o, ln1_g, ln1_b, w_router, b_router,
                                               w_gate, b_gate, w_up, b_up, w_down, b_down, ln2_g, ln2_b))
        caches = tuple(_take_layer(c, l) for c in (cache_fox_k, cache_fox_v, cache_fox_logf, cache_moba_k, cache_moba_v))
        x_all, r_p, r_s = _layer(x_all, (nb_p, t_p, nb_s, t_s), caches, page_table, lw, alpha)
        rows_p.append(r_p)
        rows_s.append(r_s)

    def stack(rows, i):
        return jnp.stack([r[i] for r in rows])

    return (x_all[:mp].reshape(nb_p, t_p, d), x_all[mp:].reshape(nb_s, t_s, d),
            stack(rows_p, 0), stack(rows_p, 1), stack(rows_p, 2), stack(rows_p, 3), stack(rows_p, 4),
            stack(rows_s, 0), stack(rows_s, 1), stack(rows_s, 2), stack(rows_s, 3), stack(rows_s, 4))
```

```python
import functools
import math

import jax
import jax.numpy as jnp
from jax import lax
from jax.experimental import pallas as pl
from jax.experimental.pallas import tpu as pltpu

F32 = jnp.float32
BF16 = jnp.bfloat16
I32 = jnp.int32

HEAD_DIM = 128
LANES = 128
MOBA_BLOCK = 256
MOBA_TOPK = 3
ROPE_THETA = 10000.0
TOP_K = 4
SWIGLU_LIMIT = 7.0
SWIGLU_ALPHA = 1.702
LN_EPS = 1e-5
EXPERT_TILE = 256
NEG = -1e30
VMEM_LIMIT = 56 * 1024 * 1024

_NT = (((1,), (1,)), ((), ()))
_TN = (((0,), (0,)), ((), ()))


def _divisor_tile(n, target, mult):
    best = None
    for t in range(mult, min(n, target) + 1, mult):
        if n % t == 0:
            best = t
    return n if best is None else best


def _params(sem):
    return pltpu.CompilerParams(dimension_semantics=sem, vmem_limit_bytes=VMEM_LIMIT)


def _split3(x):
    hi = x.astype(BF16)
    r1 = x - hi.astype(F32)
    mid = r1.astype(BF16)
    lo = (r1 - mid.astype(F32)).astype(BF16)
    return hi, mid, lo


def _layer_norm(u, g, b):
    mu = jnp.mean(u, axis=-1, keepdims=True)
    d = u - mu
    var = jnp.mean(d * d, axis=-1, keepdims=True)
    return d * lax.rsqrt(var + LN_EPS) * g + b


def _topk_lanes(vals, k):
    lane = lax.broadcasted_iota(I32, vals.shape, 1)
    out = []
    for _ in range(k):
        mx = jnp.max(vals, axis=1, keepdims=True)
        first = jnp.min(jnp.where(vals == mx, lane, vals.shape[1]), axis=1, keepdims=True)
        out.append((mx, first))
        vals = jnp.where(lane == first, -jnp.inf, vals)
    return out


def _proj_body(*refs, mode, emit_f32, emit_bf16):
    x_ref, w_ref = refs[0], refs[1]
    rest = list(refs[2:])
    acc = jnp.dot(x_ref[...], w_ref[...], preferred_element_type=F32)
    cos = sin = None
    if mode == "rope":
        cos, sin = rest.pop(0)[...], rest.pop(0)[...]
    elif mode == "logf":
        acc = jax.nn.log_sigmoid(acc + rest.pop(0)[...])
    o32 = rest.pop(0) if emit_f32 else None
    o16 = rest.pop(0) if emit_bf16 else None
    for c in range(acc.shape[1] // HEAD_DIM):
        sl = slice(c * HEAD_DIM, (c + 1) * HEAD_DIM)
        blk = acc[:, sl]
        if mode == "rope":
            blk = blk * cos + pltpu.roll(blk, HEAD_DIM // 2, 1) * sin
        if emit_f32:
            o32[:, sl] = blk
        if emit_bf16:
            o16[:, sl] = blk.astype(BF16)


def _proj(x, w, n0, n, *, tn, mode="plain", tables=None, bias=None, emit_f32=False, emit_bf16=False, name):
    m, k = x.shape
    tm = _divisor_tile(m, 1056, 16)
    assert n0 % tn == 0 and n % tn == 0
    j0 = n0 // tn
    in_specs = [pl.BlockSpec((tm, k), lambda i, j: (i, 0)), pl.BlockSpec((k, tn), lambda i, j: (0, j + j0))]
    args = [x, w]
    if mode == "rope":
        in_specs += [pl.BlockSpec((tm, HEAD_DIM), lambda i, j: (i, 0))] * 2
        args += list(tables)
    elif mode == "logf":
        in_specs.append(pl.BlockSpec((1, tn), lambda i, j: (0, j)))
        args.append(bias)
    out_shape, out_specs = [], []
    for flag, dt in ((emit_f32, F32), (emit_bf16, BF16)):
        if flag:
            out_shape.append(jax.ShapeDtypeStruct((m, n), dt))
            out_specs.append(pl.BlockSpec((tm, tn), lambda i, j: (i, j)))
    return pl.pallas_call(
        functools.partial(_proj_body, mode=mode, emit_f32=emit_f32, emit_bf16=emit_bf16),
        out_shape=out_shape, grid=(m // tm, n // tn), in_specs=in_specs, out_specs=out_specs,
        compiler_params=_params(("parallel", "arbitrary")), name=name)(*args)


def _cumsum_body(x_ref, o_ref, carry, *, tc):
    @pl.when(pl.program_id(1) == 0)
    def _():
        carry[...] = jnp.zeros_like(carry)
    x = x_ref[...]
    row = lax.broadcasted_iota(I32, (tc, tc), 0)
    col = lax.broadcasted_iota(I32, (tc, tc), 1)
    tri = (row >= col).astype(BF16)
    y = carry[...]
    for piece in _split3(x):
        y = y + jnp.dot(tri, piece, preferred_element_type=F32)
    o_ref[...] = y
    carry[...] = carry[...] + jnp.sum(x, axis=0, keepdims=True)


def _cumsum_rows(x, nb, t):
    tc = _divisor_tile(t, 512, 8)
    nc = t // tc
    return pl.pallas_call(
        functools.partial(_cumsum_body, tc=tc),
        out_shape=jax.ShapeDtypeStruct(x.shape, F32), grid=(nb, nc),
        in_specs=[pl.BlockSpec((tc, LANES), lambda b, c: (b * nc + c, 0))],
        out_specs=pl.BlockSpec((tc, LANES), lambda b, c: (b * nc + c, 0)),
        scratch_shapes=[pltpu.VMEM((1, LANES), F32)],
        compiler_params=_params(("parallel", "arbitrary")), name="cumsum_rows")(x)


def _decay_body(pt_ref, *refs, page, g):
    x_refs, o_ref, carry = refs[:g], refs[g], refs[g + 1]

    @pl.when(pl.program_id(1) == 0)
    def _():
        carry[...] = jnp.zeros_like(carry)
    row = lax.broadcasted_iota(I32, (page, page), 0)
    col = lax.broadcasted_iota(I32, (page, page), 1)
    tri = (col > row).astype(BF16)
    c = carry[...]
    for p in range(g):
        x = x_refs[p][0]
        y = jnp.broadcast_to(c, x.shape)
        for piece in _split3(x):
            y = y + jnp.dot(tri, piece, preferred_element_type=F32)
        o_ref[0, g - 1 - p] = y
        c = c + jnp.sum(x, axis=0, keepdims=True)
    carry[...] = c


def _past_decay(cache_logf, page_table):
    _, page, h = cache_logf.shape
    nb, n_pages = page_table.shape
    g = _pages_per_step(n_pages, 16)
    n_steps = n_pages // g

    def page_map(p):
        return lambda b, j, pt: (pt[b, n_pages - 1 - (j * g + p)], 0, 0)

    grid_spec = pltpu.PrefetchScalarGridSpec(
        num_scalar_prefetch=1, grid=(nb, n_steps),
        in_specs=[pl.BlockSpec((1, page, h), page_map(p)) for p in range(g)],
        out_specs=pl.BlockSpec((1, g, page, h), lambda b, j, pt: (b, n_steps - 1 - j, 0, 0)),
        scratch_shapes=[pltpu.VMEM((1, h), F32)])
    return pl.pallas_call(
        functools.partial(_decay_body, page=page, g=g),
        out_shape=jax.ShapeDtypeStruct((nb, n_pages, page, h), F32), grid_spec=grid_spec,
        compiler_params=_params(("parallel", "arbitrary")), name="past_decay")(page_table, *([cache_logf] * g))


def _online_update(zs, vs, m_sc, l_sc, acc_sc):
    m_old = m_sc[...]
    m_new = functools.reduce(jnp.maximum, [jnp.max(z, axis=1, keepdims=True) for z in zs], m_old)
    alpha = jnp.exp(m_old - m_new)
    ps = [jnp.exp(z - m_new) for z in zs]
    l_sc[...] = alpha * l_sc[...] + sum(jnp.sum(p, axis=1, keepdims=True) for p in ps)
    acc_sc[...] = alpha * acc_sc[...] + sum(jnp.dot(p.astype(BF16), v, preferred_element_type=F32)
                                            for p, v in zip(ps, vs))
    m_sc[...] = m_new


def _init_softmax(m_sc, l_sc, acc_sc):
    m_sc[...] = jnp.full_like(m_sc, -jnp.inf)
    l_sc[...] = jnp.zeros_like(l_sc)
    acc_sc[...] = jnp.zeros_like(acc_sc)


def _online_update_t(zs, vs, m_sc, l_sc, acc_sc):
    m_old = m_sc[...]
    m_new = functools.reduce(jnp.maximum, [jnp.max(z, axis=0, keepdims=True) for z in zs], m_old)
    alpha = jnp.exp(m_old - m_new)
    ps = [jnp.exp(z - m_new) for z in zs]
    l_sc[...] = alpha * l_sc[...] + sum(jnp.sum(p, axis=0, keepdims=True) for p in ps)
    acc_sc[...] = alpha * acc_sc[...] + sum(lax.dot_general(v, p.astype(BF16), _TN, preferred_element_type=F32)
                                            for p, v in zip(ps, vs))
    m_sc[...] = m_new


def _fox_prompt_body(q_ref, k_ref, v_ref, cum_ref, cumt_ref, o_ref, m_sc, l_sc, acc_sc, ck_sc, *, scale, tq):
    h, qi = pl.program_id(1), pl.program_id(2)
    _init_softmax(m_sc, l_sc, acc_sc)
    q = q_ref[...]
    cq = cumt_ref[0, pl.ds(h, 1), pl.ds(pl.multiple_of(qi * tq, tq), tq)]

    @pl.when(qi == 0)
    def _():
        pick_h = (lax.broadcasted_iota(I32, (LANES, LANES), 0) == h).astype(BF16)
        for c in range(cum_ref.shape[0] // tq):
            rows = slice(c * tq, (c + 1) * tq)
            ck_sc[rows, :] = sum(jnp.dot(piece, pick_h, preferred_element_type=F32) for piece in _split3(cum_ref[rows, :]))

    def logits(ki):
        start = pl.multiple_of(ki * tq, tq)
        st = lax.dot_general(k_ref[pl.ds(start, tq), :], q, _NT, preferred_element_type=F32)
        ck = ck_sc[pl.ds(start, tq), :]
        return (st * scale + cq) - jnp.concatenate([ck] * (tq // LANES), axis=1), v_ref[pl.ds(start, tq), :]

    def pair(j, carry):
        z0, v0 = logits(2 * j)
        z1, v1 = logits(2 * j + 1)
        _online_update_t([z0, z1], [v0, v1], m_sc, l_sc, acc_sc)
        return carry

    lax.fori_loop(0, qi // 2, pair, 0)

    @pl.when(qi % 2 == 1)
    def _():
        z, v = logits(qi - 1)
        _online_update_t([z], [v], m_sc, l_sc, acc_sc)

    z, v = logits(qi)
    key = lax.broadcasted_iota(I32, z.shape, 0)
    qry = lax.broadcasted_iota(I32, z.shape, 1)
    _online_update_t([jnp.where(key <= qry, z, NEG)], [v], m_sc, l_sc, acc_sc)
    o_ref[...] = (acc_sc[...] / l_sc[...]).T.astype(o_ref.dtype)


def _fox_prompt(q, k, v, cum, cum_t, nb, t, n_heads):
    tq = _divisor_tile(t, 512, LANES)
    nq = t // tq
    seq = lambda b, h, qi: (b, h)
    return pl.pallas_call(
        functools.partial(_fox_prompt_body, scale=HEAD_DIM ** -0.5, tq=tq),
        out_shape=jax.ShapeDtypeStruct((nb * t, n_heads * HEAD_DIM), BF16),
        grid=(nb, n_heads, nq),
        in_specs=[pl.BlockSpec((tq, HEAD_DIM), lambda b, h, qi: (b * nq + qi, h)),
                  pl.BlockSpec((t, HEAD_DIM), seq),
                  pl.BlockSpec((t, HEAD_DIM), seq),
                  pl.BlockSpec((t, LANES), lambda b, h, qi: (b, 0)),
                  pl.BlockSpec((1, n_heads, t), lambda b, h, qi: (b, 0, 0))],
        out_specs=pl.BlockSpec((tq, HEAD_DIM), lambda b, h, qi: (b * nq + qi, h)),
        scratch_shapes=[pltpu.VMEM((1, tq), F32), pltpu.VMEM((1, tq), F32), pltpu.VMEM((HEAD_DIM, tq), F32),
                        pltpu.VMEM((t, LANES), F32)],
        compiler_params=_params(("parallel", "parallel", "arbitrary")), name="fox_prompt")(q, k, v, cum, cum_t)


def _block_mean_body(k_ref, o_ref, *, blk):
    o_ref[0] = jnp.sum(k_ref[...], axis=0, keepdims=True) * (1.0 / blk)


def _block_means(k, n_blocks):
    w = k.shape[1]
    return pl.pallas_call(
        functools.partial(_block_mean_body, blk=MOBA_BLOCK),
        out_shape=jax.ShapeDtypeStruct((n_blocks, 1, w), F32), grid=(n_blocks,),
        in_specs=[pl.BlockSpec((MOBA_BLOCK, w), lambda n: (n, 0))],
        out_specs=pl.BlockSpec((1, 1, w), lambda n: (n, 0, 0)),
        compiler_params=_params(("parallel",)), name="moba_block_means")(k)


def _select_blocks(gate, n_valid):
    lane = lax.broadcasted_iota(I32, gate.shape, 1)
    picks = _topk_lanes(jnp.where(lane < n_valid, gate, -jnp.inf), MOBA_TOPK)
    sel = jnp.zeros(gate.shape, F32)
    for mx, first in picks:
        sel = jnp.where((lane == first) & (mx > -jnp.inf), 1.0, sel)
    return sel


def _moba_prompt_body(q_ref, k_ref, v_ref, km_ref, o_ref, m_sc, l_sc, acc_sc, sel_sc, *, scale, tq):
    blk = MOBA_BLOCK
    bpt = tq // blk
    nbr = sel_sc.shape[0]
    qi = pl.program_id(2)
    _init_softmax(m_sc, l_sc, acc_sc)
    q = q_ref[...]
    gate = sum(lax.dot_general(piece, q, _NT, preferred_element_type=F32) for piece in _split3(km_ref[0, :nbr, :]))
    blk_id = lax.broadcasted_iota(I32, (nbr, tq), 0)
    own = qi * bpt + lax.broadcasted_iota(I32, (1, tq), 1) // blk
    g = jnp.where(blk_id < own, gate, -jnp.inf)
    sel = jnp.zeros((nbr, tq), F32)
    for _ in range(MOBA_TOPK):
        mx = jnp.max(g, axis=0, keepdims=True)
        first = jnp.min(jnp.where(g == mx, blk_id, nbr), axis=0, keepdims=True)
        pick = (blk_id == first) & (mx > -jnp.inf)
        sel = jnp.where(pick, 1.0, sel)
        g = jnp.where(pick, -jnp.inf, g)
    sel_sc[...] = sel

    def block(n, first_q=0):
        start = pl.multiple_of(n * blk, blk)
        st = lax.dot_general(k_ref[pl.ds(start, blk), :], q_ref[first_q:, :], _NT, preferred_element_type=F32) * scale
        return st, sel_sc[pl.ds(n, 1), first_q:], v_ref[pl.ds(start, blk), :]

    def pair(j, carry):
        zs, vs = [], []
        for a in range(min(bpt, 2)):
            st, picked, v = block(min(bpt, 2) * j + a)
            zs.append(jnp.where(picked > 0.0, st, NEG))
            vs.append(v)
        _online_update_t(zs, vs, m_sc, l_sc, acc_sc)
        return carry

    lax.fori_loop(0, qi * bpt // min(bpt, 2), pair, 0)

    for a in range(bpt):
        cols = slice(a * blk, tq)
        st, picked, v = block(qi * bpt + a, a * blk)
        key = lax.broadcasted_iota(I32, st.shape, 0)
        qry = lax.broadcasted_iota(I32, st.shape, 1)
        z = jnp.where(qry < blk, jnp.where(key <= qry, st, NEG), jnp.where(picked > 0.0, st, NEG))
        _online_update_t([z], [v], m_sc.at[:, cols], l_sc.at[:, cols], acc_sc.at[:, cols])
    o_ref[...] = (acc_sc[...] / l_sc[...]).T.astype(o_ref.dtype)


def _moba_prompt(q, k, v, kmean, nb, t, n_heads):
    tq = _divisor_tile(t, 2 * MOBA_BLOCK, MOBA_BLOCK)
    nq = t // tq
    seq = lambda b, h, qi: (b, h)
    return pl.pallas_call(
        functools.partial(_moba_prompt_body, scale=HEAD_DIM ** -0.5, tq=tq),
        out_shape=jax.ShapeDtypeStruct((nb * t, n_heads * HEAD_DIM), BF16),
        grid=(nb, n_heads, nq),
        in_specs=[pl.BlockSpec((tq, HEAD_DIM), lambda b, h, qi: (b * nq + qi, h)),
                  pl.BlockSpec((t, HEAD_DIM), seq),
                  pl.BlockSpec((t, HEAD_DIM), seq),
                  pl.BlockSpec((1, LANES, HEAD_DIM), lambda b, h, qi: (b, 0, h))],
        out_specs=pl.BlockSpec((tq, HEAD_DIM), lambda b, h, qi: (b * nq + qi, h)),
        scratch_shapes=[pltpu.VMEM((1, tq), F32), pltpu.VMEM((1, tq), F32), pltpu.VMEM((HEAD_DIM, tq), F32),
                        pltpu.VMEM((-(-(t // MOBA_BLOCK) // 8) * 8, tq), F32)],
        compiler_params=_params(("parallel", "parallel", "arbitrary")), name="moba_prompt")(q, k, v, kmean)


def _head_masks(n_heads, tq, page):
    rows = jnp.arange(n_heads * tq)[:, None]
    cols = jnp.arange(page * n_heads)[None, :]
    past = jnp.where(rows // tq == cols % n_heads, 0.0, NEG).astype(F32)
    ncol = jnp.arange(tq * n_heads)[None, :]
    new = jnp.where((rows // tq == ncol % n_heads) & (ncol // n_heads <= rows % tq), 0.0, NEG).astype(F32)
    return past, new


def _fox_sample_body(pt_ref, q_ref, *refs, scale, n_steps, g):
    k_refs, v_refs = refs[:g], refs[g:2 * g]
    d_ref, cq_ref, pm_ref, kn_ref, vn_ref, dn_ref, nm_ref, o_ref, m_sc, l_sc, acc_sc, bias_sc = refs[2 * g:]
    j = pl.program_id(1)
    cq = cq_ref[0][:, :1]

    @pl.when(j == 0)
    def _():
        _init_softmax(m_sc, l_sc, acc_sc)
        bias_sc[...] = pm_ref[...] + cq

    @pl.when(j < n_steps)
    def _():
        q = q_ref[0]
        zs = [lax.dot_general(q, k_refs[p][0].astype(BF16), _NT, preferred_element_type=F32) * scale
              + bias_sc[...] + d_ref[0, p] for p in range(g)]
        _online_update(zs, [v_refs[p][0].astype(BF16) for p in range(g)], m_sc, l_sc, acc_sc)

    @pl.when(j == n_steps)
    def _():
        s = lax.dot_general(q_ref[0], kn_ref[0], _NT, preferred_element_type=F32)
        z = s * scale + nm_ref[...] + (cq - dn_ref[0])
        _online_update([z], [vn_ref[0]], m_sc, l_sc, acc_sc)
        o_ref[0] = acc_sc[...] / l_sc[...]


def _pages_per_step(n_pages, target):
    return max(c for c in range(1, target + 1) if n_pages % c == 0)


def _fox_sample(page_table, q_all, cache_k, cache_v, decay_flat, cq_b, k_new, v_new, cum_new_flat, masks):
    nb, r, _ = q_all.shape
    n_pages = page_table.shape[1]
    pw = cache_k.shape[1]
    nw = k_new.shape[1]
    pmask, nmask = masks
    g = _pages_per_step(n_pages, 4)
    n_steps = n_pages // g

    def page_map(p):
        return lambda b, j, pt: (pt[b, jnp.minimum(j, n_steps - 1) * g + p], 0, 0)

    per_b = lambda b, j, pt: (b, 0, 0)
    const = lambda b, j, pt: (0, 0)
    page_specs = [pl.BlockSpec((1, pw, HEAD_DIM), page_map(p)) for p in range(g)]
    grid_spec = pltpu.PrefetchScalarGridSpec(
        num_scalar_prefetch=1, grid=(nb, n_steps + 1),
        in_specs=[pl.BlockSpec((1, r, HEAD_DIM), per_b)] + page_specs + page_specs +
                 [pl.BlockSpec((1, g, 1, pw), lambda b, j, pt: (b, jnp.minimum(j, n_steps - 1), 0, 0)),
                  pl.BlockSpec((1, r, LANES), per_b),
                  pl.BlockSpec((r, pw), const),
                  pl.BlockSpec((1, nw, HEAD_DIM), per_b),
                  pl.BlockSpec((1, nw, HEAD_DIM), per_b),
                  pl.BlockSpec((1, 1, nw), per_b),
                  pl.BlockSpec((r, nw), const)],
        out_specs=pl.BlockSpec((1, r, HEAD_DIM), per_b),
        scratch_shapes=[pltpu.VMEM((r, 1), F32), pltpu.VMEM((r, 1), F32), pltpu.VMEM((r, HEAD_DIM), F32),
                        pltpu.VMEM((r, pw), F32)])
    return pl.pallas_call(
        functools.partial(_fox_sample_body, scale=HEAD_DIM ** -0.5, n_steps=n_steps, g=g),
        out_shape=jax.ShapeDtypeStruct((nb, r, HEAD_DIM), F32), grid_spec=grid_spec,
        compiler_params=_params(("parallel", "arbitrary")), name="fox_sample")(
            page_table, q_all, *([cache_k] * g), *([cache_v] * g), decay_flat, cq_b, pmask, k_new, v_new,
            cum_new_flat, nmask)


def _moba_sample_body(pt_ref, q_ref, *refs, scale, n_past, ppb, bps, n_heads, tq):
    npg = ppb * bps
    k_refs, v_refs = refs[:npg], refs[npg:2 * npg]
    pm_ref, kn_ref, vn_ref, nm_ref, o_ref, accs, gate_sc, mx_sc, l_sc = refs[2 * npg:]
    j = pl.program_id(1)
    q = q_ref[0]
    r = q.shape[0]
    lane = lax.broadcasted_iota(I32, (r, LANES), 1)

    def block_softmax(ks, vs, mask):
        zs = [lax.dot_general(q, kb, _NT, preferred_element_type=F32) * scale + mask for kb in ks]
        m = functools.reduce(jnp.maximum, [jnp.max(z, axis=1, keepdims=True) for z in zs])
        ps = [jnp.exp(z - m) for z in zs]
        l = sum(jnp.sum(p, axis=1, keepdims=True) for p in ps)
        acc = sum(jnp.dot(p.astype(BF16), vb, preferred_element_type=F32) for p, vb in zip(ps, vs))
        return m, l, acc

    @pl.when(j == 0)
    def _():
        gate_sc[...] = jnp.zeros_like(gate_sc)
        mx_sc[...] = jnp.zeros_like(mx_sc)
        l_sc[...] = jnp.zeros_like(l_sc)

    @pl.when(j < n_past // bps)
    def _():
        for a in range(bps):
            n = j * bps + a
            kf = [kr[0] for kr in k_refs[a * ppb:(a + 1) * ppb]]
            vb = [vr[0].astype(BF16) for vr in v_refs[a * ppb:(a + 1) * ppb]]
            m, l, acc = block_softmax([x.astype(BF16) for x in kf], vb, pm_ref[...])
            accs[n] = acc
            tok = kf[0].shape[0] // n_heads
            ksum = sum(jnp.sum(x.reshape(tok, n_heads, HEAD_DIM), axis=0) for x in kf)
            kmean = ksum * (1.0 / (tok * ppb))
            kexp = jnp.broadcast_to(kmean[:, None, :], (n_heads, tq, HEAD_DIM)).reshape(r, HEAD_DIM)
            gate = jnp.sum(q.astype(F32) * kexp, axis=1, keepdims=True)
            gate_sc[...] = jnp.where(lane == n, gate, gate_sc[...])
            mx_sc[...] = jnp.where(lane == n, m, mx_sc[...])
            l_sc[...] = jnp.where(lane == n, l, l_sc[...])

    @pl.when(j == n_past // bps)
    def _():
        m_own, l_own, acc_own = block_softmax([kn_ref[0]], [vn_ref[0]], nm_ref[...])
        sel = _select_blocks(gate_sc[...], n_past)
        m_all = jnp.maximum(jnp.max(jnp.where(sel > 0.0, mx_sc[...], -jnp.inf), axis=1, keepdims=True), m_own)
        wn = jnp.where(sel > 0.0, jnp.exp(mx_sc[...] - m_all), 0.0)
        w_own = jnp.exp(m_own - m_all)
        denom = jnp.sum(wn * l_sc[...], axis=1, keepdims=True) + w_own * l_own

        def add_block(i, num):
            wi = jnp.sum(jnp.where(lane == i, wn, 0.0), axis=1, keepdims=True)
            return num + wi * accs[i]

        num = lax.fori_loop(0, n_past, add_block, w_own * acc_own)
        o_ref[0] = num / denom


def _moba_sample(page_table, q_all, cache_k, cache_v, k_new, v_new, masks, n_heads, tq):
    nb, r, _ = q_all.shape
    n_pages = page_table.shape[1]
    pw = cache_k.shape[1]
    nw = k_new.shape[1]
    page = pw // n_heads
    ppb = MOBA_BLOCK // page
    n_past = n_pages // ppb
    pmask, nmask = masks
    bps = _pages_per_step(n_past, 2)
    npg = ppb * bps
    n_steps = n_past // bps

    def page_map(p):
        return lambda b, j, pt: (pt[b, jnp.minimum(j, n_steps - 1) * npg + p], 0, 0)

    per_b = lambda b, j, pt: (b, 0, 0)
    const = lambda b, j, pt: (0, 0)
    page_specs = [pl.BlockSpec((1, pw, HEAD_DIM), page_map(p)) for p in range(npg)]
    grid_spec = pltpu.PrefetchScalarGridSpec(
        num_scalar_prefetch=1, grid=(nb, n_steps + 1),
        in_specs=[pl.BlockSpec((1, r, HEAD_DIM), per_b)] + page_specs + page_specs +
                 [pl.BlockSpec((r, pw), const),
                  pl.BlockSpec((1, nw, HEAD_DIM), per_b),
                  pl.BlockSpec((1, nw, HEAD_DIM), per_b),
                  pl.BlockSpec((r, nw), const)],
        out_specs=pl.BlockSpec((1, r, HEAD_DIM), per_b),
        scratch_shapes=[pltpu.VMEM((n_past, r, HEAD_DIM), F32), pltpu.VMEM((r, LANES), F32),
                        pltpu.VMEM((r, LANES), F32), pltpu.VMEM((r, LANES), F32)])
    return pl.pallas_call(
        functools.partial(_moba_sample_body, scale=HEAD_DIM ** -0.5, n_past=n_past, ppb=ppb, bps=bps,
                          n_heads=n_heads, tq=tq),
        out_shape=jax.ShapeDtypeStruct((nb, r, HEAD_DIM), F32), grid_spec=grid_spec,
        compiler_params=_params(("parallel", "arbitrary")), name="moba_sample")(
            page_table, q_all, *([cache_k] * npg), *([cache_v] * npg), pmask, k_new, v_new, nmask)


def _merge_body(of_ref, om_ref, wf_ref, wm_ref, gf_ref, gm_ref, o_ref):
    br_f = jnp.dot(of_ref[...], wf_ref[...], preferred_element_type=F32)
    br_m = jnp.dot(om_ref[...], wm_ref[...], preferred_element_type=F32)
    o_ref[...] = (jax.nn.sigmoid(gf_ref[...]) * br_f + jax.nn.sigmoid(gm_ref[...]) * br_m).astype(o_ref.dtype)


def _merge(o_f, o_m, w_f, w_m, gates, d):
    m = o_f.shape[0]
    tm = _divisor_tile(m, 1056, 16)
    tn = _divisor_tile(d, 512, LANES)
    nj = d // tn
    return pl.pallas_call(
        _merge_body, out_shape=jax.ShapeDtypeStruct((m, d), BF16), grid=(m // tm, nj),
        in_specs=[pl.BlockSpec((tm, o_f.shape[1]), lambda i, j: (i, 0)),
                  pl.BlockSpec((tm, o_m.shape[1]), lambda i, j: (i, 0)),
                  pl.BlockSpec((w_f.shape[0], tn), lambda i, j: (0, j)),
                  pl.BlockSpec((w_m.shape[0], tn), lambda i, j: (0, j)),
                  pl.BlockSpec((tm, tn), lambda i, j: (i, j)),
                  pl.BlockSpec((tm, tn), lambda i, j: (i, j + nj))],
        out_specs=pl.BlockSpec((tm, tn), lambda i, j: (i, j)),
        compiler_params=_params(("parallel", "arbitrary")), name="branch_merge")(o_f, o_m, w_f, w_m, gates, gates)


def _out_ln_body(a_ref, w_ref, x_ref, g_ref, b_ref, wrh_ref, wrl_ref, br_ref, h_ref, idx_ref, wgt_ref,
                 *, alpha, n_experts):
    k = pl.program_id(1)
    part = jnp.dot(a_ref[...], w_ref[...], preferred_element_type=F32)

    @pl.when(k == 0)
    def _():
        h_ref[...] = part

    @pl.when(k > 0)
    def _():
        h_ref[...] += part

    def finish(rows):
        h = _layer_norm(alpha * x_ref[rows, :] + h_ref[rows, :], g_ref[...], b_ref[...])
        h_ref[rows, :] = h
        h_hi = h.astype(BF16)
        h_lo = (h - h_hi.astype(F32)).astype(BF16)
        logits = (jnp.dot(h_hi, wrh_ref[...], preferred_element_type=F32)
                  + jnp.dot(h_hi, wrl_ref[...], preferred_element_type=F32)
                  + jnp.dot(h_lo, wrh_ref[...], preferred_element_type=F32)) + br_ref[...]
        lane = lax.broadcasted_iota(I32, logits.shape, 1)
        picks = _topk_lanes(jnp.where(lane < n_experts, logits, -jnp.inf), TOP_K)
        idx = jnp.zeros(logits.shape, I32)
        e = jnp.zeros(logits.shape, F32)
        for r, (mx, first) in enumerate(picks):
            idx = jnp.where(lane == r, first, idx)
            e = jnp.where(lane == r, jnp.exp(mx - picks[0][0]), e)
        idx_ref[rows, :] = idx
        wgt_ref[rows, :] = e / jnp.sum(e, axis=1, keepdims=True)

    @pl.when(k == pl.num_programs(1) - 1)
    def _():
        tm = h_ref.shape[0]
        sub = _divisor_tile(tm, 176, 8)
        for r0 in range(0, tm, sub):
            finish(slice(r0, r0 + sub))


def _out_ln_route(merged, w_o, x, ln_g, ln_b, wr_hi, wr_lo, b_r, alpha, n_experts):
    m, d = x.shape
    tm = _divisor_tile(m, 352, 8)
    tk = _divisor_tile(d, 512, LANES)
    row = lambda i, k: (i, 0)
    const = lambda i, k: (0, 0)
    return pl.pallas_call(
        functools.partial(_out_ln_body, alpha=alpha, n_experts=n_experts),
        out_shape=[jax.ShapeDtypeStruct((m, d), F32), jax.ShapeDtypeStruct((m, LANES), I32),
                   jax.ShapeDtypeStruct((m, LANES), F32)],
        grid=(m // tm, d // tk),
        in_specs=[pl.BlockSpec((tm, tk), lambda i, k: (i, k)), pl.BlockSpec((tk, d), lambda i, k: (k, 0)),
                  pl.BlockSpec((tm, d), row), pl.BlockSpec((1, d), const), pl.BlockSpec((1, d), const),
                  pl.BlockSpec((d, LANES), const), pl.BlockSpec((d, LANES), const), pl.BlockSpec((1, LANES), const)],
        out_specs=[pl.BlockSpec((tm, d), row), pl.BlockSpec((tm, LANES), row), pl.BlockSpec((tm, LANES), row)],
        compiler_params=_params(("parallel", "arbitrary")), name="out_proj_ln_router")(
            merged, w_o, x, ln_g, ln_b, wr_hi, wr_lo, b_r)


def _gather_rows_body(tok_ref, src_ref, nu_ref, h_hbm, o_ref, buf, sem, *, tm):
    i = pl.program_id(0)
    n_used = nu_ref[0]
    last = tok_ref.shape[0] - 1

    def fetch(tile):
        slot = tile % 2
        src = src_ref[tile]

        def issue(r, c):
            tok = tok_ref[jnp.minimum(src + r, last)]
            pltpu.make_async_copy(h_hbm.at[pl.ds(tok, 1)], buf.at[slot, pl.ds(r, 1)], sem.at[slot]).start()
            return c

        lax.fori_loop(0, tm, issue, 0)

    @pl.when((i == 0) & (n_used > 0))
    def _():
        fetch(0)

    @pl.when(i + 1 < n_used)
    def _():
        fetch(i + 1)

    @pl.when(i < n_used)
    def _():
        slot = i % 2
        pltpu.make_async_copy(h_hbm.at[pl.ds(0, tm)], buf.at[slot], sem.at[slot]).wait()
        o_ref[...] = buf[slot].astype(o_ref.dtype)

    @pl.when(i >= n_used)
    def _():
        o_ref[...] = jnp.zeros_like(o_ref)


def _gather_rows(h, sorted_tok, tile_src, n_used, tm):
    n_tiles = tile_src.shape[0]
    d = h.shape[1]
    grid_spec = pltpu.PrefetchScalarGridSpec(
        num_scalar_prefetch=3, grid=(n_tiles,),
        in_specs=[pl.BlockSpec(memory_space=pl.ANY)],
        out_specs=pl.BlockSpec((tm, d), lambda i, tok, src, nu: (i, 0)),
        scratch_shapes=[pltpu.VMEM((2, tm, d), F32), pltpu.SemaphoreType.DMA((2,))])
    return pl.pallas_call(
        functools.partial(_gather_rows_body, tm=tm),
        out_shape=jax.ShapeDtypeStruct((n_tiles * tm, d), BF16), grid_spec=grid_spec,
        compiler_params=_params(("arbitrary",)), name="moe_gather_rows")(sorted_tok, tile_src, n_used, h)


def _expert_changed(te_ref, i):
    return (i == 0) | (te_ref[i] != te_ref[jnp.maximum(i - 1, 0)])


def _gate_up_body(te_ref, nu_ref, x_ref, wg_ref, wu_ref, bg_ref, bu_ref, o_ref, wg_bf, wu_bf):
    i = pl.program_id(1)
    used = i < nu_ref[0]

    @pl.when(used & _expert_changed(te_ref, i))
    def _():
        wg_bf[...] = wg_ref[0].astype(BF16)
        wu_bf[...] = wu_ref[0].astype(BF16)

    @pl.when(used)
    def _():
        x = x_ref[...]
        gate = jnp.dot(x, wg_bf[...], preferred_element_type=F32) + bg_ref[0]
        up = jnp.dot(x, wu_bf[...], preferred_element_type=F32) + bu_ref[0]
        gate = jnp.minimum(gate, SWIGLU_LIMIT)
        up = jnp.clip(up, -SWIGLU_LIMIT, SWIGLU_LIMIT)
        glu = gate * jax.nn.sigmoid(gate * SWIGLU_ALPHA)
        o_ref[...] = ((up + 1.0) * glu).astype(o_ref.dtype)

    @pl.when(jnp.logical_not(used))
    def _():
        o_ref[...] = jnp.zeros_like(o_ref)


def _gate_up(xs, tile_e, n_used, w_gate, b_gate, w_up, b_up, tm):
    n_rows, d = xs.shape
    f = w_gate.shape[2]
    tf = _divisor_tile(f, 512, LANES)
    wmap = lambda j, i, te, nu: (te[i], 0, j)
    grid_spec = pltpu.PrefetchScalarGridSpec(
        num_scalar_prefetch=2, grid=(f // tf, n_rows // tm),
        in_specs=[pl.BlockSpec((tm, d), lambda j, i, te, nu: (i, 0)),
                  pl.BlockSpec((1, d, tf), wmap), pl.BlockSpec((1, d, tf), wmap),
                  pl.BlockSpec((1, 1, tf), wmap), pl.BlockSpec((1, 1, tf), wmap)],
        out_specs=pl.BlockSpec((tm, tf), lambda j, i, te, nu: (i, j)),
        scratch_shapes=[pltpu.VMEM((d, tf), BF16), pltpu.VMEM((d, tf), BF16)])
    return pl.pallas_call(
        _gate_up_body, out_shape=jax.ShapeDtypeStruct((n_rows, f), BF16), grid_spec=grid_spec,
        compiler_params=_params(("arbitrary", "arbitrary")), name="moe_gate_up")(
            tile_e, n_used, xs, w_gate, w_up, b_gate, b_up)


def _down_body(te_ref, nu_ref, a_ref, w_ref, b_ref, o_ref, w_bf):
    i = pl.program_id(1)
    used = i < nu_ref[0]

    @pl.when(used & _expert_changed(te_ref, i))
    def _():
        w_bf[...] = w_ref[0].astype(BF16)

    @pl.when(used)
    def _():
        o_ref[...] = jnp.dot(a_ref[...], w_bf[...], preferred_element_type=F32) + b_ref[0]

    @pl.when(jnp.logical_not(used))
    def _():
        o_ref[...] = jnp.zeros_like(o_ref)


def _down(act, tile_e, n_used, w_down, b_down, tm):
    n_rows, f = act.shape
    d = w_down.shape[2]
    tn = _divisor_tile(d, 1024, LANES)
    wmap = lambda j, i, te, nu: (te[i], 0, j)
    grid_spec = pltpu.PrefetchScalarGridSpec(
        num_scalar_prefetch=2, grid=(d // tn, n_rows // tm),
        in_specs=[pl.BlockSpec((tm, f), lambda j, i, te, nu: (i, 0)),
                  pl.BlockSpec((1, f, tn), wmap), pl.BlockSpec((1, 1, tn), wmap)],
        out_specs=pl.BlockSpec((tm, tn), lambda j, i, te, nu: (i, j)),
        scratch_shapes=[pltpu.VMEM((f, tn), BF16)])
    return pl.pallas_call(
        _down_body, out_shape=jax.ShapeDtypeStruct((n_rows, d), F32), grid_spec=grid_spec,
        compiler_params=_params(("arbitrary", "arbitrary")), name="moe_down")(tile_e, n_used, act, w_down, b_down)


def _combine_ln_body(pos_ref, y_hbm, h_ref, w_ref, g_ref, b_ref, o_ref, buf, sem, *, tc, alpha):
    i = pl.program_id(0)

    def fetch(tile):
        slot = tile % 2
        base = tile * tc * TOP_K

        def issue(r, c):
            for k in range(TOP_K):
                pltpu.make_async_copy(y_hbm.at[pl.ds(pos_ref[base + r * TOP_K + k], 1)],
                                      buf.at[slot, k, pl.ds(r, 1)], sem.at[slot]).start()
            return c

        lax.fori_loop(0, tc, issue, 0)

    @pl.when(i == 0)
    def _():
        fetch(0)

    @pl.when(i + 1 < pl.num_programs(0))
    def _():
        fetch(i + 1)

    slot = i % 2
    for k in range(TOP_K):
        pltpu.make_async_copy(y_hbm.at[pl.ds(0, tc)], buf.at[slot, k], sem.at[slot]).wait()
    w = w_ref[...]
    f = sum(w[:, k:k + 1] * buf[slot, k] for k in range(TOP_K))
    o_ref[...] = _layer_norm(alpha * h_ref[...] + f, g_ref[...], b_ref[...])


def _combine_ln(y_rows, pos, h, w_pad, ln_g, ln_b, alpha):
    m, d = h.shape
    tc = _divisor_tile(m, 128, 8)
    row = lambda i, p: (i, 0)
    const = lambda i, p: (0, 0)
    grid_spec = pltpu.PrefetchScalarGridSpec(
        num_scalar_prefetch=1, grid=(m // tc,),
        in_specs=[pl.BlockSpec(memory_space=pl.ANY), pl.BlockSpec((tc, d), row), pl.BlockSpec((tc, LANES), row),
                  pl.BlockSpec((1, d), const), pl.BlockSpec((1, d), const)],
        out_specs=pl.BlockSpec((tc, d), row),
        scratch_shapes=[pltpu.VMEM((2, TOP_K, tc, d), F32), pltpu.SemaphoreType.DMA((2,))])
    return pl.pallas_call(
        functools.partial(_combine_ln_body, tc=tc, alpha=alpha),
        out_shape=jax.ShapeDtypeStruct((m, d), F32), grid_spec=grid_spec,
        compiler_params=_params(("arbitrary",)), name="moe_combine_ln")(pos, y_rows, h, w_pad, ln_g, ln_b)


def _moe_ln(h, top_idx, w_pad, w_gate, b_gate, w_up, b_up, w_down, b_down, ln_g, ln_b, alpha):
    m, d = h.shape
    n_experts = w_gate.shape[0]
    tm = EXPERT_TILE
    n_assign = m * TOP_K
    flat_e = top_idx.reshape(-1)
    order = jnp.argsort(flat_e)
    rank = jnp.argsort(order)
    experts = jnp.arange(n_experts, dtype=I32)
    is_e = flat_e[:, None] == experts[None, :]
    counts = jnp.sum(is_e, axis=0, dtype=I32)
    padded = (counts + tm - 1) // tm * tm
    pend = jnp.cumsum(padded)
    shift = (pend - padded) - (jnp.cumsum(counts) - counts)
    pos = (rank + jnp.sum(jnp.where(is_e, shift[None, :], 0), axis=1)).astype(I32)
    n_tiles = -(-(n_assign + n_experts * (tm - 1)) // tm)
    tile_row = jnp.arange(n_tiles, dtype=I32) * tm
    tile_e = jnp.minimum(jnp.sum(pend[None, :] <= tile_row[:, None], axis=1), n_experts - 1).astype(I32)
    tile_src = (tile_row - jnp.sum(jnp.where(tile_e[:, None] == experts[None, :], shift[None, :], 0), axis=1)).astype(I32)
    n_used = (pend[-1] // tm).astype(I32).reshape(1)

    xs = _gather_rows(h, (order // TOP_K).astype(I32), tile_src, n_used, tm)
    act = _gate_up(xs, tile_e, n_used, w_gate, b_gate[:, None, :], w_up, b_up[:, None, :], tm)
    y_rows = _down(act, tile_e, n_used, w_down, b_down[:, None, :], tm)
    return _combine_ln(y_rows, pos, h, w_pad, ln_g, ln_b, alpha)


def _rope_tables(pos):
    half = HEAD_DIM // 2
    inv = ROPE_THETA ** (-jnp.arange(half, dtype=F32) / half)
    ang = pos.astype(F32)[:, None] * inv[None, :]
    cos, sin = jnp.cos(ang), jnp.sin(ang)
    return jnp.concatenate([cos, cos], axis=1), jnp.concatenate([-sin, sin], axis=1)


def _layer(x_all, dims, caches, page_table, lw, alpha):
    (w_in, b_forget, w_br_fox, w_br_moba, w_o, ln1_g, ln1_b, w_router, b_router,
     w_gate, b_gate, w_up, b_up, w_down, b_down, ln2_g, ln2_b) = lw
    nb_p, t_p, nb_s, t_s = dims
    cache_fk, cache_fv, cache_flog, cache_mk, cache_mv = caches
    m, d = x_all.shape
    hf = b_forget.shape[0]
    wf = hf * HEAD_DIM
    wm = (w_in.shape[1] - 3 * wf - hf - 2 * d) // 3
    hm = wm // HEAD_DIM
    mp = nb_p * t_p
    n_pool, page = cache_fk.shape[0], cache_fk.shape[1]
    past_len = page_table.shape[1] * page
    assert t_p % MOBA_BLOCK == 0 and past_len % MOBA_BLOCK == 0 and MOBA_BLOCK % page == 0 and t_s <= MOBA_BLOCK
    assert t_p // MOBA_BLOCK <= LANES and past_len // MOBA_BLOCK <= LANES and hf <= LANES

    w_head = w_in[:, :3 * wf].astype(BF16)
    w_flin = jnp.pad(w_in[:, 3 * wf:3 * wf + hf], ((0, 0), (0, LANES - hf))).astype(BF16)
    w_tail = w_in[:, 3 * wf + hf:].astype(BF16)
    b_f = jnp.pad(b_forget, (0, LANES - hf))[None, :]
    x_bf = x_all.astype(BF16)
    pos = jnp.concatenate([jnp.tile(jnp.arange(t_p), nb_p), past_len + jnp.tile(jnp.arange(t_s), nb_s)])
    tables = _rope_tables(pos)

    tn = _divisor_tile(wf, 512, LANES)
    (q_f,) = _proj(x_bf, w_head, 0, wf, tn=tn, emit_bf16=True, name="proj_fox_q")
    k_f, k_f16 = _proj(x_bf, w_head, wf, wf, tn=tn, emit_f32=True, emit_bf16=True, name="proj_fox_k")
    v_f, v_f16 = _proj(x_bf, w_head, 2 * wf, wf, tn=tn, emit_f32=True, emit_bf16=True, name="proj_fox_v")
    (logf_pad,) = _proj(x_bf, w_flin, 0, LANES, tn=LANES, mode="logf", bias=b_f, emit_f32=True, name="proj_fox_logf")
    tn = _divisor_tile(wm, 512, LANES)
    (q_m,) = _proj(x_bf, w_tail, 0, wm, tn=tn, mode="rope", tables=tables, emit_bf16=True, name="proj_moba_q")
    k_m, k_m16 = _proj(x_bf, w_tail, wm, wm, tn=tn, mode="rope", tables=tables, emit_f32=True, emit_bf16=True,
                       name="proj_moba_k")
    v_m, v_m16 = _proj(x_bf, w_tail, 2 * wm, wm, tn=tn, emit_f32=True, emit_bf16=True, name="proj_moba_v")
    (gates,) = _proj(x_bf, w_tail, 3 * wm, 2 * d, tn=_divisor_tile(math.gcd(3 * wm, d), 512, LANES), emit_f32=True,
                     name="proj_gates")

    cum_p = _cumsum_rows(logf_pad[:mp], nb_p, t_p)
    cum_t = cum_p[:, :hf].reshape(nb_p, t_p, hf).transpose(0, 2, 1)
    o_f_p = _fox_prompt(q_f, k_f16, v_f16, cum_p, cum_t, nb_p, t_p, hf)

    nblk = t_p // MOBA_BLOCK
    kmean = _block_means(k_m, nb_p * nblk).reshape(nb_p, nblk, wm)
    kmean = jnp.pad(kmean, ((0, 0), (0, LANES - nblk), (0, 0)))
    o_m_p = _moba_prompt(q_m, k_m16, v_m16, kmean, nb_p, t_p, hm)

    def q_rows(q, h):
        return q[mp:].reshape(nb_s, t_s, h, HEAD_DIM).transpose(0, 2, 1, 3).reshape(nb_s, h * t_s, HEAD_DIM)

    def kv_rows(a, h):
        return a[mp:].reshape(nb_s, t_s * h, HEAD_DIM)

    def out_rows(o, h):
        return o.reshape(nb_s, h, t_s, HEAD_DIM).transpose(0, 2, 1, 3).reshape(nb_s * t_s, h * HEAD_DIM).astype(BF16)

    t_pad = -(-t_s // LANES) * LANES
    logf_s = jnp.pad(logf_pad[mp:].reshape(nb_s, t_s, LANES), ((0, 0), (0, t_pad - t_s), (0, 0)))
    cum_s = _cumsum_rows(logf_s.reshape(nb_s * t_pad, LANES), nb_s, t_pad).reshape(nb_s, t_pad, LANES)[:, :t_s, :hf]
    decay = _past_decay(cache_flog, page_table).reshape(nb_s, page_table.shape[1], 1, page * hf)
    cq_b = jnp.broadcast_to(cum_s.transpose(0, 2, 1).reshape(nb_s, hf * t_s, 1), (nb_s, hf * t_s, LANES))
    o_f_s = _fox_sample(page_table, q_rows(q_f, hf), cache_fk.reshape(n_pool, page * hf, HEAD_DIM),
                        cache_fv.reshape(n_pool, page * hf, HEAD_DIM), decay, cq_b,
                        kv_rows(k_f16, hf), kv_rows(v_f16, hf), cum_s.reshape(nb_s, 1, t_s * hf),
                        _head_masks(hf, t_s, page))
    o_m_s = _moba_sample(page_table, q_rows(q_m, hm), cache_mk.reshape(n_pool, page * hm, HEAD_DIM),
                         cache_mv.reshape(n_pool, page * hm, HEAD_DIM), kv_rows(k_m16, hm), kv_rows(v_m16, hm),
                         _head_masks(hm, t_s, page), hm, t_s)

    o_f = jnp.concatenate([o_f_p, out_rows(o_f_s, hf)], axis=0)
    o_m = jnp.concatenate([o_m_p, out_rows(o_m_s, hm)], axis=0)

    merged = _merge(o_f, o_m, w_br_fox.astype(BF16), w_br_moba.astype(BF16), gates, d)
    n_experts = w_router.shape[1]
    wr = jnp.pad(w_router, ((0, 0), (0, LANES - n_experts)))
    wr_hi = wr.astype(BF16)
    wr_lo = (wr - wr_hi.astype(F32)).astype(BF16)
    b_r = jnp.pad(b_router, (0, LANES - n_experts))[None, :]
    h, idx_pad, w_pad = _out_ln_route(merged, w_o.astype(BF16), x_all, ln1_g[None, :], ln1_b[None, :],
                                      wr_hi, wr_lo, b_r, alpha, n_experts)

    y = _moe_ln(h, idx_pad[:, :TOP_K], w_pad, w_gate, b_gate, w_up, b_up, w_down, b_down,
                ln2_g[None, :], ln2_b[None, :], alpha)

    def split_rows(a, h):
        return (a[:mp].reshape(nb_p, t_p, h, HEAD_DIM), a[mp:].reshape(nb_s, t_s, h, HEAD_DIM))

    logf = logf_pad[:, :hf]
    rows_p, rows_s = zip(split_rows(k_f, hf), split_rows(v_f, hf),
                         (logf[:mp].reshape(nb_p, t_p, hf), logf[mp:].reshape(nb_s, t_s, hf)),
                         split_rows(k_m, hm), split_rows(v_m, hm))
    return y, rows_p, rows_s


def _take_layer(a, l):
    return a.reshape(a.shape[1:]) if a.shape[0] == 1 else a[l]


def kernel(x_prompt, x_sample, cache_fox_k, cache_fox_v, cache_fox_logf, cache_moba_k, cache_moba_v, page_table, w_in, b_forget, w_br_fox, w_br_moba, w_o, ln1_g, ln1_b, w_router, b_router, w_gate, b_gate, w_up, b_up, w_down, b_down, ln2_g, ln2_b):
    depth = w_in.shape[0]
    alpha = (2.0 * depth) ** 0.25
    nb_p, t_p, d = x_prompt.shape
    nb_s, t_s, _ = x_sample.shape
    mp = nb_p * t_p
    x_all = jnp.concatenate([x_prompt.reshape(mp, d), x_sample.reshape(nb_s * t_s, d)], axis=0)
    rows_p, rows_s = [], []
    for l in range(depth):
        lw = tuple(_take_layer(w, l) for w in (w_in, b_forget, w_br_fox, w_br_moba, w_o, ln1_g, ln1_b, w_router, b_router,
                                               w_gate, b_gate, w_up, b_up, w_down, b_down, ln2_g, ln2_b))
        caches = tuple(_take_layer(c, l) for c in (cache_fox_k, cache_fox_v, cache_fox_logf, cache_moba_k, cache_moba_v))
        x_all, r_p, r_s = _layer(x_all, (nb_p, t_p, nb_s, t_s), caches, page_table, lw, alpha)
        rows_p.append(r_p)
        rows_s.append(r_s)

    def stack(rows, i):
        return jnp.stack([r[i] for r in rows])

    return (x_all[:mp].reshape(nb_p, t_p, d), x_all[mp:].reshape(nb_s, t_s, d),
            stack(rows_p, 0), stack(rows_p, 1), stack(rows_p, 2), stack(rows_p, 3), stack(rows_p, 4),
            stack(rows_s, 0), stack(rows_s, 1), stack(rows_s, 2), stack(rows_s, 3), stack(rows_s, 4))
```

```python
import functools
import math

import jax
import jax.numpy as jnp
from jax import lax
from jax.experimental import pallas as pl
from jax.experimental.pallas import tpu as pltpu

F32 = jnp.float32
BF16 = jnp.bfloat16
I32 = jnp.int32

HEAD_DIM = 128
LANES = 128
MOBA_BLOCK = 256
MOBA_TOPK = 3
ROPE_THETA = 10000.0
TOP_K = 4
SWIGLU_LIMIT = 7.0
SWIGLU_ALPHA = 1.702
LN_EPS = 1e-5
EXPERT_TILE = 512
NEG = -1e30
VMEM_LIMIT = 56 * 1024 * 1024

_NT = (((1,), (1,)), ((), ()))
_TN = (((0,), (0,)), ((), ()))


def _divisor_tile(n, target, mult):
    best = None
    for t in range(mult, min(n, target) + 1, mult):
        if n % t == 0:
            best = t
    return n if best is None else best


def _params(sem):
    return pltpu.CompilerParams(dimension_semantics=sem, vmem_limit_bytes=VMEM_LIMIT)


def _split3(x):
    hi = x.astype(BF16)
    r1 = x - hi.astype(F32)
    mid = r1.astype(BF16)
    lo = (r1 - mid.astype(F32)).astype(BF16)
    return hi, mid, lo


def _layer_norm(u, g, b):
    mu = jnp.mean(u, axis=-1, keepdims=True)
    d = u - mu
    var = jnp.mean(d * d, axis=-1, keepdims=True)
    return d * lax.rsqrt(var + LN_EPS) * g + b


def _topk_lanes(vals, k):
    lane = lax.broadcasted_iota(I32, vals.shape, 1)
    out = []
    for _ in range(k):
        mx = jnp.max(vals, axis=1, keepdims=True)
        first = jnp.min(jnp.where(vals == mx, lane, vals.shape[1]), axis=1, keepdims=True)
        out.append((mx, first))
        vals = jnp.where(lane == first, -jnp.inf, vals)
    return out


def _proj_body(*refs, mode, emit_f32, emit_bf16):
    x_ref, w_ref = refs[0], refs[1]
    rest = list(refs[2:])
    acc = jnp.dot(x_ref[...], w_ref[...], preferred_element_type=F32)
    cos = sin = None
    if mode == "rope":
        cos, sin = rest.pop(0)[...], rest.pop(0)[...]
    elif mode == "logf":
        acc = jax.nn.log_sigmoid(acc + rest.pop(0)[...])
    o32 = rest.pop(0) if emit_f32 else None
    o16 = rest.pop(0) if emit_bf16 else None
    for c in range(acc.shape[1] // HEAD_DIM):
        sl = slice(c * HEAD_DIM, (c + 1) * HEAD_DIM)
        blk = acc[:, sl]
        if mode == "rope":
            blk = blk * cos + pltpu.roll(blk, HEAD_DIM // 2, 1) * sin
        if emit_f32:
            o32[:, sl] = blk
        if emit_bf16:
            o16[:, sl] = blk.astype(BF16)


def _proj(x, w, n0, n, *, tn, mode="plain", tables=None, bias=None, emit_f32=False, emit_bf16=False, name):
    m, k = x.shape
    tm = _divisor_tile(m, 1056, 16)
    assert n0 % tn == 0 and n % tn == 0
    j0 = n0 // tn
    in_specs = [pl.BlockSpec((tm, k), lambda i, j: (i, 0)), pl.BlockSpec((k, tn), lambda i, j: (0, j + j0))]
    args = [x, w]
    if mode == "rope":
        in_specs += [pl.BlockSpec((tm, HEAD_DIM), lambda i, j: (i, 0))] * 2
        args += list(tables)
    elif mode == "logf":
        in_specs.append(pl.BlockSpec((1, tn), lambda i, j: (0, j)))
        args.append(bias)
    out_shape, out_specs = [], []
    for flag, dt in ((emit_f32, F32), (emit_bf16, BF16)):
        if flag:
            out_shape.append(jax.ShapeDtypeStruct((m, n), dt))
            out_specs.append(pl.BlockSpec((tm, tn), lambda i, j: (i, j)))
    return pl.pallas_call(
        functools.partial(_proj_body, mode=mode, emit_f32=emit_f32, emit_bf16=emit_bf16),
        out_shape=out_shape, grid=(m // tm, n // tn), in_specs=in_specs, out_specs=out_specs,
        compiler_params=_params(("parallel", "arbitrary")), name=name)(*args)


def _cumsum_body(x_ref, o_ref, carry, *, tc):
    @pl.when(pl.program_id(1) == 0)
    def _():
        carry[...] = jnp.zeros_like(carry)
    x = x_ref[...]
    row = lax.broadcasted_iota(I32, (tc, tc), 0)
    col = lax.broadcasted_iota(I32, (tc, tc), 1)
    tri = (row >= col).astype(BF16)
    y = carry[...]
    for piece in _split3(x):
        y = y + jnp.dot(tri, piece, preferred_element_type=F32)
    o_ref[...] = y
    carry[...] = carry[...] + jnp.sum(x, axis=0, keepdims=True)


def _cumsum_rows(x, nb, t):
    tc = _divisor_tile(t, 512, 8)
    nc = t // tc
    return pl.pallas_call(
        functools.partial(_cumsum_body, tc=tc),
        out_shape=jax.ShapeDtypeStruct(x.shape, F32), grid=(nb, nc),
        in_specs=[pl.BlockSpec((tc, LANES), lambda b, c: (b * nc + c, 0))],
        out_specs=pl.BlockSpec((tc, LANES), lambda b, c: (b * nc + c, 0)),
        scratch_shapes=[pltpu.VMEM((1, LANES), F32)],
        compiler_params=_params(("parallel", "arbitrary")), name="cumsum_rows")(x)


def _decay_body(pt_ref, *refs, page, g):
    x_refs, o_ref, carry = refs[:g], refs[g], refs[g + 1]

    @pl.when(pl.program_id(1) == 0)
    def _():
        carry[...] = jnp.zeros_like(carry)
    row = lax.broadcasted_iota(I32, (page, page), 0)
    col = lax.broadcasted_iota(I32, (page, page), 1)
    tri = (col > row).astype(BF16)
    c = carry[...]
    for p in range(g):
        x = x_refs[p][0]
        y = jnp.broadcast_to(c, x.shape)
        for piece in _split3(x):
            y = y + jnp.dot(tri, piece, preferred_element_type=F32)
        o_ref[0, g - 1 - p] = y
        c = c + jnp.sum(x, axis=0, keepdims=True)
    carry[...] = c


def _past_decay(cache_logf, page_table):
    _, page, h = cache_logf.shape
    nb, n_pages = page_table.shape
    g = _pages_per_step(n_pages, 16)
    n_steps = n_pages // g

    def page_map(p):
        return lambda b, j, pt: (pt[b, n_pages - 1 - (j * g + p)], 0, 0)

    grid_spec = pltpu.PrefetchScalarGridSpec(
        num_scalar_prefetch=1, grid=(nb, n_steps),
        in_specs=[pl.BlockSpec((1, page, h), page_map(p)) for p in range(g)],
        out_specs=pl.BlockSpec((1, g, page, h), lambda b, j, pt: (b, n_steps - 1 - j, 0, 0)),
        scratch_shapes=[pltpu.VMEM((1, h), F32)])
    return pl.pallas_call(
        functools.partial(_decay_body, page=page, g=g),
        out_shape=jax.ShapeDtypeStruct((nb, n_pages, page, h), F32), grid_spec=grid_spec,
        compiler_params=_params(("parallel", "arbitrary")), name="past_decay")(page_table, *([cache_logf] * g))


def _online_update(zs, vs, m_sc, l_sc, acc_sc):
    m_old = m_sc[...]
    m_new = functools.reduce(jnp.maximum, [jnp.max(z, axis=1, keepdims=True) for z in zs], m_old)
    alpha = jnp.exp(m_old - m_new)
    ps = [jnp.exp(z - m_new) for z in zs]
    l_sc[...] = alpha * l_sc[...] + sum(jnp.sum(p, axis=1, keepdims=True) for p in ps)
    acc_sc[...] = alpha * acc_sc[...] + sum(jnp.dot(p.astype(BF16), v, preferred_element_type=F32)
                                            for p, v in zip(ps, vs))
    m_sc[...] = m_new


def _init_softmax(m_sc, l_sc, acc_sc):
    m_sc[...] = jnp.full_like(m_sc, -jnp.inf)
    l_sc[...] = jnp.zeros_like(l_sc)
    acc_sc[...] = jnp.zeros_like(acc_sc)


def _online_update_t(zs, vs, m_sc, l_sc, acc_sc):
    m_old = m_sc[...]
    m_new = functools.reduce(jnp.maximum, [jnp.max(z, axis=0, keepdims=True) for z in zs], m_old)
    alpha = jnp.exp(m_old - m_new)
    ps = [jnp.exp(z - m_new) for z in zs]
    l_sc[...] = alpha * l_sc[...] + sum(jnp.sum(p, axis=0, keepdims=True) for p in ps)
    acc_sc[...] = alpha * acc_sc[...] + sum(lax.dot_general(v, p.astype(BF16), _TN, preferred_element_type=F32)
                                            for p, v in zip(ps, vs))
    m_sc[...] = m_new


def _fox_prompt_body(q_ref, k_ref, v_ref, cum_ref, cumt_ref, o_ref, m_sc, l_sc, acc_sc, ck_sc, *, scale, tq):
    h, qi = pl.program_id(1), pl.program_id(2)
    _init_softmax(m_sc, l_sc, acc_sc)
    q = q_ref[...]
    cq = cumt_ref[0, pl.ds(h, 1), pl.ds(pl.multiple_of(qi * tq, tq), tq)]

    @pl.when(qi == 0)
    def _():
        pick_h = (lax.broadcasted_iota(I32, (LANES, LANES), 0) == h).astype(BF16)
        for c in range(cum_ref.shape[0] // tq):
            rows = slice(c * tq, (c + 1) * tq)
            ck_sc[rows, :] = sum(jnp.dot(piece, pick_h, preferred_element_type=F32) for piece in _split3(cum_ref[rows, :]))

    def logits(ki):
        start = pl.multiple_of(ki * tq, tq)
        st = lax.dot_general(k_ref[pl.ds(start, tq), :], q, _NT, preferred_element_type=F32)
        ck = ck_sc[pl.ds(start, tq), :]
        return (st * scale + cq) - jnp.concatenate([ck] * (tq // LANES), axis=1), v_ref[pl.ds(start, tq), :]

    def pair(j, carry):
        z0, v0 = logits(2 * j)
        z1, v1 = logits(2 * j + 1)
        _online_update_t([z0, z1], [v0, v1], m_sc, l_sc, acc_sc)
        return carry

    lax.fori_loop(0, qi // 2, pair, 0)

    @pl.when(qi % 2 == 1)
    def _():
        z, v = logits(qi - 1)
        _online_update_t([z], [v], m_sc, l_sc, acc_sc)

    z, v = logits(qi)
    key = lax.broadcasted_iota(I32, z.shape, 0)
    qry = lax.broadcasted_iota(I32, z.shape, 1)
    _online_update_t([jnp.where(key <= qry, z, NEG)], [v], m_sc, l_sc, acc_sc)
    o_ref[...] = (acc_sc[...] / l_sc[...]).T.astype(o_ref.dtype)


def _fox_prompt(q, k, v, cum, cum_t, nb, t, n_heads):
    tq = _divisor_tile(t, 512, LANES)
    nq = t // tq
    seq = lambda b, h, qi: (b, h)
    return pl.pallas_call(
        functools.partial(_fox_prompt_body, scale=HEAD_DIM ** -0.5, tq=tq),
        out_shape=jax.ShapeDtypeStruct((nb * t, n_heads * HEAD_DIM), BF16),
        grid=(nb, n_heads, nq),
        in_specs=[pl.BlockSpec((tq, HEAD_DIM), lambda b, h, qi: (b * nq + qi, h)),
                  pl.BlockSpec((t, HEAD_DIM), seq),
                  pl.BlockSpec((t, HEAD_DIM), seq),
                  pl.BlockSpec((t, LANES), lambda b, h, qi: (b, 0)),
                  pl.BlockSpec((1, n_heads, t), lambda b, h, qi: (b, 0, 0))],
        out_specs=pl.BlockSpec((tq, HEAD_DIM), lambda b, h, qi: (b * nq + qi, h)),
        scratch_shapes=[pltpu.VMEM((1, tq), F32), pltpu.VMEM((1, tq), F32), pltpu.VMEM((HEAD_DIM, tq), F32),
                        pltpu.VMEM((t, LANES), F32)],
        compiler_params=_params(("parallel", "parallel", "arbitrary")), name="fox_prompt")(q, k, v, cum, cum_t)


def _block_mean_body(k_ref, o_ref, *, blk):
    o_ref[0] = jnp.sum(k_ref[...], axis=0, keepdims=True) * (1.0 / blk)


def _block_means(k, n_blocks):
    w = k.shape[1]
    return pl.pallas_call(
        functools.partial(_block_mean_body, blk=MOBA_BLOCK),
        out_shape=jax.ShapeDtypeStruct((n_blocks, 1, w), F32), grid=(n_blocks,),
        in_specs=[pl.BlockSpec((MOBA_BLOCK, w), lambda n: (n, 0))],
        out_specs=pl.BlockSpec((1, 1, w), lambda n: (n, 0, 0)),
        compiler_params=_params(("parallel",)), name="moba_block_means")(k)


def _select_blocks(gate, n_valid):
    lane = lax.broadcasted_iota(I32, gate.shape, 1)
    picks = _topk_lanes(jnp.where(lane < n_valid, gate, -jnp.inf), MOBA_TOPK)
    sel = jnp.zeros(gate.shape, F32)
    for mx, first in picks:
        sel = jnp.where((lane == first) & (mx > -jnp.inf), 1.0, sel)
    return sel


def _moba_prompt_body(q_ref, k_ref, v_ref, km_ref, o_ref, m_sc, l_sc, acc_sc, sel_sc, *, scale, tq):
    blk = MOBA_BLOCK
    bpt = tq // blk
    nbr = sel_sc.shape[0]
    qi = pl.program_id(2)
    _init_softmax(m_sc, l_sc, acc_sc)
    q = q_ref[...]
    gate = sum(lax.dot_general(piece, q, _NT, preferred_element_type=F32) for piece in _split3(km_ref[0, :nbr, :]))
    blk_id = lax.broadcasted_iota(I32, (nbr, tq), 0)
    own = qi * bpt + lax.broadcasted_iota(I32, (1, tq), 1) // blk
    g = jnp.where(blk_id < own, gate, -jnp.inf)
    sel = jnp.zeros((nbr, tq), F32)
    for _ in range(MOBA_TOPK):
        mx = jnp.max(g, axis=0, keepdims=True)
        first = jnp.min(jnp.where(g == mx, blk_id, nbr), axis=0, keepdims=True)
        pick = (blk_id == first) & (mx > -jnp.inf)
        sel = jnp.where(pick, 1.0, sel)
        g = jnp.where(pick, -jnp.inf, g)
    sel_sc[...] = sel

    def block(n, first_q=0):
        start = pl.multiple_of(n * blk, blk)
        st = lax.dot_general(k_ref[pl.ds(start, blk), :], q_ref[first_q:, :], _NT, preferred_element_type=F32) * scale
        return st, sel_sc[pl.ds(n, 1), first_q:], v_ref[pl.ds(start, blk), :]

    def pair(j, carry):
        zs, vs = [], []
        for a in range(min(bpt, 2)):
            st, picked, v = block(min(bpt, 2) * j + a)
            zs.append(jnp.where(picked > 0.0, st, NEG))
            vs.append(v)
        _online_update_t(zs, vs, m_sc, l_sc, acc_sc)
        return carry

    lax.fori_loop(0, qi * bpt // min(bpt, 2), pair, 0)

    for a in range(bpt):
        cols = slice(a * blk, tq)
        st, picked, v = block(qi * bpt + a, a * blk)
        key = lax.broadcasted_iota(I32, st.shape, 0)
        qry = lax.broadcasted_iota(I32, st.shape, 1)
        z = jnp.where(qry < blk, jnp.where(key <= qry, st, NEG), jnp.where(picked > 0.0, st, NEG))
        _online_update_t([z], [v], m_sc.at[:, cols], l_sc.at[:, cols], acc_sc.at[:, cols])
    o_ref[...] = (acc_sc[...] / l_sc[...]).T.astype(o_ref.dtype)


def _moba_prompt(q, k, v, kmean, nb, t, n_heads):
    tq = _divisor_tile(t, 2 * MOBA_BLOCK, MOBA_BLOCK)
    nq = t // tq
    seq = lambda b, h, qi: (b, h)
    return pl.pallas_call(
        functools.partial(_moba_prompt_body, scale=HEAD_DIM ** -0.5, tq=tq),
        out_shape=jax.ShapeDtypeStruct((nb * t, n_heads * HEAD_DIM), BF16),
        grid=(nb, n_heads, nq),
        in_specs=[pl.BlockSpec((tq, HEAD_DIM), lambda b, h, qi: (b * nq + qi, h)),
                  pl.BlockSpec((t, HEAD_DIM), seq),
                  pl.BlockSpec((t, HEAD_DIM), seq),
                  pl.BlockSpec((1, LANES, HEAD_DIM), lambda b, h, qi: (b, 0, h))],
        out_specs=pl.BlockSpec((tq, HEAD_DIM), lambda b, h, qi: (b * nq + qi, h)),
        scratch_shapes=[pltpu.VMEM((1, tq), F32), pltpu.VMEM((1, tq), F32), pltpu.VMEM((HEAD_DIM, tq), F32),
                        pltpu.VMEM((-(-(t // MOBA_BLOCK) // 8) * 8, tq), F32)],
        compiler_params=_params(("parallel", "parallel", "arbitrary")), name="moba_prompt")(q, k, v, kmean)


def _head_masks(n_heads, tq, page):
    rows = jnp.arange(n_heads * tq)[:, None]
    cols = jnp.arange(page * n_heads)[None, :]
    past = jnp.where(rows // tq == cols % n_heads, 0.0, NEG).astype(F32)
    ncol = jnp.arange(tq * n_heads)[None, :]
    new = jnp.where((rows // tq == ncol % n_heads) & (ncol // n_heads <= rows % tq), 0.0, NEG).astype(F32)
    return past, new


def _fox_sample_body(pt_ref, q_ref, *refs, scale, n_steps, g):
    k_refs, v_refs = refs[:g], refs[g:2 * g]
    d_ref, cq_ref, pm_ref, kn_ref, vn_ref, dn_ref, nm_ref, o_ref, m_sc, l_sc, acc_sc, bias_sc = refs[2 * g:]
    j = pl.program_id(1)
    cq = cq_ref[0][:, :1]

    @pl.when(j == 0)
    def _():
        _init_softmax(m_sc, l_sc, acc_sc)
        bias_sc[...] = pm_ref[...] + cq

    @pl.when(j < n_steps)
    def _():
        q = q_ref[0]
        zs = [lax.dot_general(q, k_refs[p][0].astype(BF16), _NT, preferred_element_type=F32) * scale
              + bias_sc[...] + d_ref[0, p] for p in range(g)]
        _online_update(zs, [v_refs[p][0].astype(BF16) for p in range(g)], m_sc, l_sc, acc_sc)

    @pl.when(j == n_steps)
    def _():
        s = lax.dot_general(q_ref[0], kn_ref[0], _NT, preferred_element_type=F32)
        z = s * scale + nm_ref[...] + (cq - dn_ref[0])
        _online_update([z], [vn_ref[0]], m_sc, l_sc, acc_sc)
        o_ref[0] = acc_sc[...] / l_sc[...]


def _pages_per_step(n_pages, target):
    return max(c for c in range(1, target + 1) if n_pages % c == 0)


def _fox_sample(page_table, q_all, cache_k, cache_v, decay_flat, cq_b, k_new, v_new, cum_new_flat, masks):
    nb, r, _ = q_all.shape
    n_pages = page_table.shape[1]
    pw = cache_k.shape[1]
    nw = k_new.shape[1]
    pmask, nmask = masks
    g = _pages_per_step(n_pages, 4)
    n_steps = n_pages // g

    def page_map(p):
        return lambda b, j, pt: (pt[b, jnp.minimum(j, n_steps - 1) * g + p], 0, 0)

    per_b = lambda b, j, pt: (b, 0, 0)
    const = lambda b, j, pt: (0, 0)
    page_specs = [pl.BlockSpec((1, pw, HEAD_DIM), page_map(p)) for p in range(g)]
    grid_spec = pltpu.PrefetchScalarGridSpec(
        num_scalar_prefetch=1, grid=(nb, n_steps + 1),
        in_specs=[pl.BlockSpec((1, r, HEAD_DIM), per_b)] + page_specs + page_specs +
                 [pl.BlockSpec((1, g, 1, pw), lambda b, j, pt: (b, jnp.minimum(j, n_steps - 1), 0, 0)),
                  pl.BlockSpec((1, r, LANES), per_b),
                  pl.BlockSpec((r, pw), const),
                  pl.BlockSpec((1, nw, HEAD_DIM), per_b),
                  pl.BlockSpec((1, nw, HEAD_DIM), per_b),
                  pl.BlockSpec((1, 1, nw), per_b),
                  pl.BlockSpec((r, nw), const)],
        out_specs=pl.BlockSpec((1, r, HEAD_DIM), per_b),
        scratch_shapes=[pltpu.VMEM((r, 1), F32), pltpu.VMEM((r, 1), F32), pltpu.VMEM((r, HEAD_DIM), F32),
                        pltpu.VMEM((r, pw), F32)])
    return pl.pallas_call(
        functools.partial(_fox_sample_body, scale=HEAD_DIM ** -0.5, n_steps=n_steps, g=g),
        out_shape=jax.ShapeDtypeStruct((nb, r, HEAD_DIM), F32), grid_spec=grid_spec,
        compiler_params=_params(("parallel", "arbitrary")), name="fox_sample")(
            page_table, q_all, *([cache_k] * g), *([cache_v] * g), decay_flat, cq_b, pmask, k_new, v_new,
            cum_new_flat, nmask)


def _moba_sample_body(pt_ref, q_ref, *refs, scale, n_past, ppb, bps, n_heads, tq):
    npg = ppb * bps
    k_refs, v_refs = refs[:npg], refs[npg:2 * npg]
    pm_ref, kn_ref, vn_ref, nm_ref, o_ref, accs, gate_sc, mx_sc, l_sc = refs[2 * npg:]
    j = pl.program_id(1)
    q = q_ref[0]
    r = q.shape[0]
    lane = lax.broadcasted_iota(I32, (r, LANES), 1)

    def block_softmax(ks, vs, mask):
        zs = [lax.dot_general(q, kb, _NT, preferred_element_type=F32) * scale + mask for kb in ks]
        m = functools.reduce(jnp.maximum, [jnp.max(z, axis=1, keepdims=True) for z in zs])
        ps = [jnp.exp(z - m) for z in zs]
        l = sum(jnp.sum(p, axis=1, keepdims=True) for p in ps)
        acc = sum(jnp.dot(p.astype(BF16), vb, preferred_element_type=F32) for p, vb in zip(ps, vs))
        return m, l, acc

    @pl.when(j == 0)
    def _():
        gate_sc[...] = jnp.zeros_like(gate_sc)
        mx_sc[...] = jnp.zeros_like(mx_sc)
        l_sc[...] = jnp.zeros_like(l_sc)

    @pl.when(j < n_past // bps)
    def _():
        for a in range(bps):
            n = j * bps + a
            kf = [kr[0] for kr in k_refs[a * ppb:(a + 1) * ppb]]
            vb = [vr[0].astype(BF16) for vr in v_refs[a * ppb:(a + 1) * ppb]]
            m, l, acc = block_softmax([x.astype(BF16) for x in kf], vb, pm_ref[...])
            accs[n] = acc
            tok = kf[0].shape[0] // n_heads
            ksum = sum(jnp.sum(x.reshape(tok, n_heads, HEAD_DIM), axis=0) for x in kf)
            kmean = ksum * (1.0 / (tok * ppb))
            kexp = jnp.broadcast_to(kmean[:, None, :], (n_heads, tq, HEAD_DIM)).reshape(r, HEAD_DIM)
            gate = jnp.sum(q.astype(F32) * kexp, axis=1, keepdims=True)
            gate_sc[...] = jnp.where(lane == n, gate, gate_sc[...])
            mx_sc[...] = jnp.where(lane == n, m, mx_sc[...])
            l_sc[...] = jnp.where(lane == n, l, l_sc[...])

    @pl.when(j == n_past // bps)
    def _():
        m_own, l_own, acc_own = block_softmax([kn_ref[0]], [vn_ref[0]], nm_ref[...])
        sel = _select_blocks(gate_sc[...], n_past)
        m_all = jnp.maximum(jnp.max(jnp.where(sel > 0.0, mx_sc[...], -jnp.inf), axis=1, keepdims=True), m_own)
        wn = jnp.where(sel > 0.0, jnp.exp(mx_sc[...] - m_all), 0.0)
        w_own = jnp.exp(m_own - m_all)
        denom = jnp.sum(wn * l_sc[...], axis=1, keepdims=True) + w_own * l_own

        def add_block(i, num):
            wi = jnp.sum(jnp.where(lane == i, wn, 0.0), axis=1, keepdims=True)
            return num + wi * accs[i]

        num = lax.fori_loop(0, n_past, add_block, w_own * acc_own)
        o_ref[0] = num / denom


def _moba_sample(page_table, q_all, cache_k, cache_v, k_new, v_new, masks, n_heads, tq):
    nb, r, _ = q_all.shape
    n_pages = page_table.shape[1]
    pw = cache_k.shape[1]
    nw = k_new.shape[1]
    page = pw // n_heads
    ppb = MOBA_BLOCK // page
    n_past = n_pages // ppb
    pmask, nmask = masks
    bps = _pages_per_step(n_past, 2)
    npg = ppb * bps
    n_steps = n_past // bps

    def page_map(p):
        return lambda b, j, pt: (pt[b, jnp.minimum(j, n_steps - 1) * npg + p], 0, 0)

    per_b = lambda b, j, pt: (b, 0, 0)
    const = lambda b, j, pt: (0, 0)
    page_specs = [pl.BlockSpec((1, pw, HEAD_DIM), page_map(p)) for p in range(npg)]
    grid_spec = pltpu.PrefetchScalarGridSpec(
        num_scalar_prefetch=1, grid=(nb, n_steps + 1),
        in_specs=[pl.BlockSpec((1, r, HEAD_DIM), per_b)] + page_specs + page_specs +
                 [pl.BlockSpec((r, pw), const),
                  pl.BlockSpec((1, nw, HEAD_DIM), per_b),
                  pl.BlockSpec((1, nw, HEAD_DIM), per_b),
                  pl.BlockSpec((r, nw), const)],
        out_specs=pl.BlockSpec((1, r, HEAD_DIM), per_b),
        scratch_shapes=[pltpu.VMEM((n_past, r, HEAD_DIM), F32), pltpu.VMEM((r, LANES), F32),
                        pltpu.VMEM((r, LANES), F32), pltpu.VMEM((r, LANES), F32)])
    return pl.pallas_call(
        functools.partial(_moba_sample_body, scale=HEAD_DIM ** -0.5, n_past=n_past, ppb=ppb, bps=bps,
                          n_heads=n_heads, tq=tq),
        out_shape=jax.ShapeDtypeStruct((nb, r, HEAD_DIM), F32), grid_spec=grid_spec,
        compiler_params=_params(("parallel", "arbitrary")), name="moba_sample")(
            page_table, q_all, *([cache_k] * npg), *([cache_v] * npg), pmask, k_new, v_new, nmask)


def _merge_body(of_ref, om_ref, wf_ref, wm_ref, gf_ref, gm_ref, o_ref):
    br_f = jnp.dot(of_ref[...], wf_ref[...], preferred_element_type=F32)
    br_m = jnp.dot(om_ref[...], wm_ref[...], preferred_element_type=F32)
    o_ref[...] = (jax.nn.sigmoid(gf_ref[...]) * br_f + jax.nn.sigmoid(gm_ref[...]) * br_m).astype(o_ref.dtype)


def _merge(o_f, o_m, w_f, w_m, gates, d):
    m = o_f.shape[0]
    tm = _divisor_tile(m, 1056, 16)
    tn = _divisor_tile(d, 512, LANES)
    nj = d // tn
    return pl.pallas_call(
        _merge_body, out_shape=jax.ShapeDtypeStruct((m, d), BF16), grid=(m // tm, nj),
        in_specs=[pl.BlockSpec((tm, o_f.shape[1]), lambda i, j: (i, 0)),
                  pl.BlockSpec((tm, o_m.shape[1]), lambda i, j: (i, 0)),
                  pl.BlockSpec((w_f.shape[0], tn), lambda i, j: (0, j)),
                  pl.BlockSpec((w_m.shape[0], tn), lambda i, j: (0, j)),
                  pl.BlockSpec((tm, tn), lambda i, j: (i, j)),
                  pl.BlockSpec((tm, tn), lambda i, j: (i, j + nj))],
        out_specs=pl.BlockSpec((tm, tn), lambda i, j: (i, j)),
        compiler_params=_params(("parallel", "arbitrary")), name="branch_merge")(o_f, o_m, w_f, w_m, gates, gates)


def _out_ln_body(a_ref, w_ref, x_ref, g_ref, b_ref, wrh_ref, wrl_ref, br_ref, h_ref, idx_ref, wgt_ref,
                 *, alpha, n_experts):
    k = pl.program_id(1)
    part = jnp.dot(a_ref[...], w_ref[...], preferred_element_type=F32)

    @pl.when(k == 0)
    def _():
        h_ref[...] = part

    @pl.when(k > 0)
    def _():
        h_ref[...] += part

    def finish(rows):
        h = _layer_norm(alpha * x_ref[rows, :] + h_ref[rows, :], g_ref[...], b_ref[...])
        h_ref[rows, :] = h
        h_hi = h.astype(BF16)
        h_lo = (h - h_hi.astype(F32)).astype(BF16)
        logits = (jnp.dot(h_hi, wrh_ref[...], preferred_element_type=F32)
                  + jnp.dot(h_hi, wrl_ref[...], preferred_element_type=F32)
                  + jnp.dot(h_lo, wrh_ref[...], preferred_element_type=F32)) + br_ref[...]
        lane = lax.broadcasted_iota(I32, logits.shape, 1)
        picks = _topk_lanes(jnp.where(lane < n_experts, logits, -jnp.inf), TOP_K)
        idx = jnp.zeros(logits.shape, I32)
        e = jnp.zeros(logits.shape, F32)
        for r, (mx, first) in enumerate(picks):
            idx = jnp.where(lane == r, first, idx)
            e = jnp.where(lane == r, jnp.exp(mx - picks[0][0]), e)
        idx_ref[rows, :] = idx
        wgt_ref[rows, :] = e / jnp.sum(e, axis=1, keepdims=True)

    @pl.when(k == pl.num_programs(1) - 1)
    def _():
        tm = h_ref.shape[0]
        sub = _divisor_tile(tm, 176, 8)
        for r0 in range(0, tm, sub):
            finish(slice(r0, r0 + sub))


def _out_ln_route(merged, w_o, x, ln_g, ln_b, wr_hi, wr_lo, b_r, alpha, n_experts):
    m, d = x.shape
    tm = _divisor_tile(m, 264, 8)
    tk = _divisor_tile(d, 1024, LANES)
    row = lambda i, k: (i, 0)
    const = lambda i, k: (0, 0)
    return pl.pallas_call(
        functools.partial(_out_ln_body, alpha=alpha, n_experts=n_experts),
        out_shape=[jax.ShapeDtypeStruct((m, d), F32), jax.ShapeDtypeStruct((m, LANES), I32),
                   jax.ShapeDtypeStruct((m, LANES), F32)],
        grid=(m // tm, d // tk),
        in_specs=[pl.BlockSpec((tm, tk), lambda i, k: (i, k)), pl.BlockSpec((tk, d), lambda i, k: (k, 0)),
                  pl.BlockSpec((tm, d), row), pl.BlockSpec((1, d), const), pl.BlockSpec((1, d), const),
                  pl.BlockSpec((d, LANES), const), pl.BlockSpec((d, LANES), const), pl.BlockSpec((1, LANES), const)],
        out_specs=[pl.BlockSpec((tm, d), row), pl.BlockSpec((tm, LANES), row), pl.BlockSpec((tm, LANES), row)],
        compiler_params=_params(("parallel", "arbitrary")), name="out_proj_ln_router")(
            merged, w_o, x, ln_g, ln_b, wr_hi, wr_lo, b_r)


def _gather_rows_body(tok_ref, src_ref, nu_ref, h_hbm, o_ref, buf, sem, *, tm):
    i = pl.program_id(0)
    n_used = nu_ref[0]
    last = tok_ref.shape[0] - 1

    def fetch(tile):
        slot = tile % 2
        src = src_ref[tile]

        def issue(r, c):
            tok = tok_ref[jnp.minimum(src + r, last)]
            pltpu.make_async_copy(h_hbm.at[pl.ds(tok, 1)], buf.at[slot, pl.ds(r, 1)], sem.at[slot]).start()
            return c

        lax.fori_loop(0, tm, issue, 0)

    @pl.when((i == 0) & (n_used > 0))
    def _():
        fetch(0)

    @pl.when(i + 1 < n_used)
    def _():
        fetch(i + 1)

    @pl.when(i < n_used)
    def _():
        slot = i % 2
        pltpu.make_async_copy(h_hbm.at[pl.ds(0, tm)], buf.at[slot], sem.at[slot]).wait()
        o_ref[...] = buf[slot].astype(o_ref.dtype)

    @pl.when(i >= n_used)
    def _():
        o_ref[...] = jnp.zeros_like(o_ref)


def _gather_rows(h, sorted_tok, tile_src, n_used, tm):
    n_tiles = tile_src.shape[0]
    d = h.shape[1]
    grid_spec = pltpu.PrefetchScalarGridSpec(
        num_scalar_prefetch=3, grid=(n_tiles,),
        in_specs=[pl.BlockSpec(memory_space=pl.ANY)],
        out_specs=pl.BlockSpec((tm, d), lambda i, tok, src, nu: (i, 0)),
        scratch_shapes=[pltpu.VMEM((2, tm, d), F32), pltpu.SemaphoreType.DMA((2,))])
    return pl.pallas_call(
        functools.partial(_gather_rows_body, tm=tm),
        out_shape=jax.ShapeDtypeStruct((n_tiles * tm, d), BF16), grid_spec=grid_spec,
        compiler_params=_params(("arbitrary",)), name="moe_gather_rows")(sorted_tok, tile_src, n_used, h)


def _expert_changed(te_ref, i):
    return (i == 0) | (te_ref[i] != te_ref[jnp.maximum(i - 1, 0)])


def _gate_up_body(te_ref, nu_ref, x_ref, wg_ref, wu_ref, bg_ref, bu_ref, o_ref, wg_bf, wu_bf):
    i = pl.program_id(1)
    used = i < nu_ref[0]

    @pl.when(used & _expert_changed(te_ref, i))
    def _():
        wg_bf[...] = wg_ref[0].astype(BF16)
        wu_bf[...] = wu_ref[0].astype(BF16)

    @pl.when(used)
    def _():
        x = x_ref[...]
        gate = jnp.dot(x, wg_bf[...], preferred_element_type=F32) + bg_ref[0]
        up = jnp.dot(x, wu_bf[...], preferred_element_type=F32) + bu_ref[0]
        gate = jnp.minimum(gate, SWIGLU_LIMIT)
        up = jnp.clip(up, -SWIGLU_LIMIT, SWIGLU_LIMIT)
        glu = gate * jax.nn.sigmoid(gate * SWIGLU_ALPHA)
        o_ref[...] = ((up + 1.0) * glu).astype(o_ref.dtype)

    @pl.when(jnp.logical_not(used))
    def _():
        o_ref[...] = jnp.zeros_like(o_ref)


def _gate_up(xs, tile_e, n_used, w_gate, b_gate, w_up, b_up, tm):
    n_rows, d = xs.shape
    f = w_gate.shape[2]
    tf = _divisor_tile(f, 512, LANES)
    wmap = lambda j, i, te, nu: (te[i], 0, j)
    grid_spec = pltpu.PrefetchScalarGridSpec(
        num_scalar_prefetch=2, grid=(f // tf, n_rows // tm),
        in_specs=[pl.BlockSpec((tm, d), lambda j, i, te, nu: (i, 0)),
                  pl.BlockSpec((1, d, tf), wmap), pl.BlockSpec((1, d, tf), wmap),
                  pl.BlockSpec((1, 1, tf), wmap), pl.BlockSpec((1, 1, tf), wmap)],
        out_specs=pl.BlockSpec((tm, tf), lambda j, i, te, nu: (i, j)),
        scratch_shapes=[pltpu.VMEM((d, tf), BF16), pltpu.VMEM((d, tf), BF16)])
    return pl.pallas_call(
        _gate_up_body, out_shape=jax.ShapeDtypeStruct((n_rows, f), BF16), grid_spec=grid_spec,
        compiler_params=_params(("arbitrary", "arbitrary")), name="moe_gate_up")(
            tile_e, n_used, xs, w_gate, w_up, b_gate, b_up)


def _down_body(te_ref, nu_ref, a_ref, w_ref, b_ref, o_ref, w_bf):
    i = pl.program_id(1)
    used = i < nu_ref[0]

    @pl.when(used & _expert_changed(te_ref, i))
    def _():
        w_bf[...] = w_ref[0].astype(BF16)

    @pl.when(used)
    def _():
        o_ref[...] = jnp.dot(a_ref[...], w_bf[...], preferred_element_type=F32) + b_ref[0]

    @pl.when(jnp.logical_not(used))
    def _():
        o_ref[...] = jnp.zeros_like(o_ref)


def _down(act, tile_e, n_used, w_down, b_down, tm):
    n_rows, f = act.shape
    d = w_down.shape[2]
    tn = _divisor_tile(d, 1024, LANES)
    wmap = lambda j, i, te, nu: (te[i], 0, j)
    grid_spec = pltpu.PrefetchScalarGridSpec(
        num_scalar_prefetch=2, grid=(d // tn, n_rows // tm),
        in_specs=[pl.BlockSpec((tm, f), lambda j, i, te, nu: (i, 0)),
                  pl.BlockSpec((1, f, tn), wmap), pl.BlockSpec((1, 1, tn), wmap)],
        out_specs=pl.BlockSpec((tm, tn), lambda j, i, te, nu: (i, j)),
        scratch_shapes=[pltpu.VMEM((f, tn), BF16)])
    return pl.pallas_call(
        _down_body, out_shape=jax.ShapeDtypeStruct((n_rows, d), F32), grid_spec=grid_spec,
        compiler_params=_params(("arbitrary", "arbitrary")), name="moe_down")(tile_e, n_used, act, w_down, b_down)


def _combine_ln_body(pos_ref, y_hbm, h_ref, w_ref, g_ref, b_ref, o_ref, buf, sem, *, tc, alpha):
    i = pl.program_id(0)

    def fetch(tile):
        slot = tile % 2
        base = tile * tc * TOP_K

        def issue(r, c):
            for k in range(TOP_K):
                pltpu.make_async_copy(y_hbm.at[pl.ds(pos_ref[base + r * TOP_K + k], 1)],
                                      buf.at[slot, k, pl.ds(r, 1)], sem.at[slot]).start()
            return c

        lax.fori_loop(0, tc, issue, 0)

    @pl.when(i == 0)
    def _():
        fetch(0)

    @pl.when(i + 1 < pl.num_programs(0))
    def _():
        fetch(i + 1)

    slot = i % 2
    for k in range(TOP_K):
        pltpu.make_async_copy(y_hbm.at[pl.ds(0, tc)], buf.at[slot, k], sem.at[slot]).wait()
    w = w_ref[...]
    f = sum(w[:, k:k + 1] * buf[slot, k] for k in range(TOP_K))
    o_ref[...] = _layer_norm(alpha * h_ref[...] + f, g_ref[...], b_ref[...])


def _combine_ln(y_rows, pos, h, w_pad, ln_g, ln_b, alpha):
    m, d = h.shape
    tc = _divisor_tile(m, 128, 8)
    row = lambda i, p: (i, 0)
    const = lambda i, p: (0, 0)
    grid_spec = pltpu.PrefetchScalarGridSpec(
        num_scalar_prefetch=1, grid=(m // tc,),
        in_specs=[pl.BlockSpec(memory_space=pl.ANY), pl.BlockSpec((tc, d), row), pl.BlockSpec((tc, LANES), row),
                  pl.BlockSpec((1, d), const), pl.BlockSpec((1, d), const)],
        out_specs=pl.BlockSpec((tc, d), row),
        scratch_shapes=[pltpu.VMEM((2, TOP_K, tc, d), F32), pltpu.SemaphoreType.DMA((2,))])
    return pl.pallas_call(
        functools.partial(_combine_ln_body, tc=tc, alpha=alpha),
        out_shape=jax.ShapeDtypeStruct((m, d), F32), grid_spec=grid_spec,
        compiler_params=_params(("arbitrary",)), name="moe_combine_ln")(pos, y_rows, h, w_pad, ln_g, ln_b)


def _moe_ln(h, top_idx, w_pad, w_gate, b_gate, w_up, b_up, w_down, b_down, ln_g, ln_b, alpha):
    m, d = h.shape
    n_experts = w_gate.shape[0]
    tm = EXPERT_TILE
    n_assign = m * TOP_K
    flat_e = top_idx.reshape(-1)
    order = jnp.argsort(flat_e)
    rank = jnp.argsort(order)
    experts = jnp.arange(n_experts, dtype=I32)
    is_e = flat_e[:, None] == experts[None, :]
    counts = jnp.sum(is_e, axis=0, dtype=I32)
    padded = (counts + tm - 1) // tm * tm
    pend = jnp.cumsum(padded)
    shift = (pend - padded) - (jnp.cumsum(counts) - counts)
    pos = (rank + jnp.sum(jnp.where(is_e, shift[None, :], 0), axis=1)).astype(I32)
    n_tiles = -(-(n_assign + n_experts * (tm - 1)) // tm)
    tile_row = jnp.arange(n_tiles, dtype=I32) * tm
    tile_e = jnp.minimum(jnp.sum(pend[None, :] <= tile_row[:, None], axis=1), n_experts - 1).astype(I32)
    tile_src = (tile_row - jnp.sum(jnp.where(tile_e[:, None] == experts[None, :], shift[None, :], 0), axis=1)).astype(I32)
    n_used = (pend[-1] // tm).astype(I32).reshape(1)

    xs = _gather_rows(h, (order // TOP_K).astype(I32), tile_src, n_used, tm)
    act = _gate_up(xs, tile_e, n_used, w_gate, b_gate[:, None, :], w_up, b_up[:, None, :], tm)
    y_rows = _down(act, tile_e, n_used, w_down, b_down[:, None, :], tm)
    return _combine_ln(y_rows, pos, h, w_pad, ln_g, ln_b, alpha)


def _rope_tables(pos):
    half = HEAD_DIM // 2
    inv = ROPE_THETA ** (-jnp.arange(half, dtype=F32) / half)
    ang = pos.astype(F32)[:, None] * inv[None, :]
    cos, sin = jnp.cos(ang), jnp.sin(ang)
    return jnp.concatenate([cos, cos], axis=1), jnp.concatenate([-sin, sin], axis=1)


def _layer(x_all, dims, caches, page_table, lw, alpha):
    (w_in, b_forget, w_br_fox, w_br_moba, w_o, ln1_g, ln1_b, w_router, b_router,
     w_gate, b_gate, w_up, b_up, w_down, b_down, ln2_g, ln2_b) = lw
    nb_p, t_p, nb_s, t_s = dims
    cache_fk, cache_fv, cache_flog, cache_mk, cache_mv = caches
    m, d = x_all.shape
    hf = b_forget.shape[0]
    wf = hf * HEAD_DIM
    wm = (w_in.shape[1] - 3 * wf - hf - 2 * d) // 3
    hm = wm // HEAD_DIM
    mp = nb_p * t_p
    n_pool, page = cache_fk.shape[0], cache_fk.shape[1]
    past_len = page_table.shape[1] * page
    assert t_p % MOBA_BLOCK == 0 and past_len % MOBA_BLOCK == 0 and MOBA_BLOCK % page == 0 and t_s <= MOBA_BLOCK
    assert t_p // MOBA_BLOCK <= LANES and past_len // MOBA_BLOCK <= LANES and hf <= LANES

    w_head = w_in[:, :3 * wf].astype(BF16)
    w_flin = jnp.pad(w_in[:, 3 * wf:3 * wf + hf], ((0, 0), (0, LANES - hf))).astype(BF16)
    w_tail = w_in[:, 3 * wf + hf:].astype(BF16)
    b_f = jnp.pad(b_forget, (0, LANES - hf))[None, :]
    x_bf = x_all.astype(BF16)
    pos = jnp.concatenate([jnp.tile(jnp.arange(t_p), nb_p), past_len + jnp.tile(jnp.arange(t_s), nb_s)])
    tables = _rope_tables(pos)

    tn = _divisor_tile(wf, 512, LANES)
    (q_f,) = _proj(x_bf, w_head, 0, wf, tn=tn, emit_bf16=True, name="proj_fox_q")
    k_f, k_f16 = _proj(x_bf, w_head, wf, wf, tn=tn, emit_f32=True, emit_bf16=True, name="proj_fox_k")
    v_f, v_f16 = _proj(x_bf, w_head, 2 * wf, wf, tn=tn, emit_f32=True, emit_bf16=True, name="proj_fox_v")
    (logf_pad,) = _proj(x_bf, w_flin, 0, LANES, tn=LANES, mode="logf", bias=b_f, emit_f32=True, name="proj_fox_logf")
    tn = _divisor_tile(wm, 512, LANES)
    (q_m,) = _proj(x_bf, w_tail, 0, wm, tn=tn, mode="rope", tables=tables, emit_bf16=True, name="proj_moba_q")
    k_m, k_m16 = _proj(x_bf, w_tail, wm, wm, tn=tn, mode="rope", tables=tables, emit_f32=True, emit_bf16=True,
                       name="proj_moba_k")
    v_m, v_m16 = _proj(x_bf, w_tail, 2 * wm, wm, tn=tn, emit_f32=True, emit_bf16=True, name="proj_moba_v")
    (gates,) = _proj(x_bf, w_tail, 3 * wm, 2 * d, tn=_divisor_tile(math.gcd(3 * wm, d), 512, LANES), emit_f32=True,
                     name="proj_gates")

    cum_p = _cumsum_rows(logf_pad[:mp], nb_p, t_p)
    cum_t = cum_p[:, :hf].reshape(nb_p, t_p, hf).transpose(0, 2, 1)
    o_f_p = _fox_prompt(q_f, k_f16, v_f16, cum_p, cum_t, nb_p, t_p, hf)

    nblk = t_p // MOBA_BLOCK
    kmean = _block_means(k_m, nb_p * nblk).reshape(nb_p, nblk, wm)
    kmean = jnp.pad(kmean, ((0, 0), (0, LANES - nblk), (0, 0)))
    o_m_p = _moba_prompt(q_m, k_m16, v_m16, kmean, nb_p, t_p, hm)

    def q_rows(q, h):
        return q[mp:].reshape(nb_s, t_s, h, HEAD_DIM).transpose(0, 2, 1, 3).reshape(nb_s, h * t_s, HEAD_DIM)

    def kv_rows(a, h):
        return a[mp:].reshape(nb_s, t_s * h, HEAD_DIM)

    def out_rows(o, h):
        return o.reshape(nb_s, h, t_s, HEAD_DIM).transpose(0, 2, 1, 3).reshape(nb_s * t_s, h * HEAD_DIM).astype(BF16)

    t_pad = -(-t_s // LANES) * LANES
    logf_s = jnp.pad(logf_pad[mp:].reshape(nb_s, t_s, LANES), ((0, 0), (0, t_pad - t_s), (0, 0)))
    cum_s = _cumsum_rows(logf_s.reshape(nb_s * t_pad, LANES), nb_s, t_pad).reshape(nb_s, t_pad, LANES)[:, :t_s, :hf]
    decay = _past_decay(cache_flog, page_table).reshape(nb_s, page_table.shape[1], 1, page * hf)
    cq_b = jnp.broadcast_to(cum_s.transpose(0, 2, 1).reshape(nb_s, hf * t_s, 1), (nb_s, hf * t_s, LANES))
    o_f_s = _fox_sample(page_table, q_rows(q_f, hf), cache_fk.reshape(n_pool, page * hf, HEAD_DIM),
                        cache_fv.reshape(n_pool, page * hf, HEAD_DIM), decay, cq_b,
                        kv_rows(k_f16, hf), kv_rows(v_f16, hf), cum_s.reshape(nb_s, 1, t_s * hf),
                        _head_masks(hf, t_s, page))
    o_m_s = _moba_sample(page_table, q_rows(q_m, hm), cache_mk.reshape(n_pool, page * hm, HEAD_DIM),
                         cache_mv.reshape(n_pool, page * hm, HEAD_DIM), kv_rows(k_m16, hm), kv_rows(v_m16, hm),
                         _head_masks(hm, t_s, page), hm, t_s)

    o_f = jnp.concatenate([o_f_p, out_rows(o_f_s, hf)], axis=0)
    o_m = jnp.concatenate([o_m_p, out_rows(o_m_s, hm)], axis=0)

    merged = _merge(o_f, o_m, w_br_fox.astype(BF16), w_br_moba.astype(BF16), gates, d)
    n_experts = w_router.shape[1]
    wr = jnp.pad(w_router, ((0, 0), (0, LANES - n_experts)))
    wr_hi = wr.astype(BF16)
    wr_lo = (wr - wr_hi.astype(F32)).astype(BF16)
    b_r = jnp.pad(b_router, (0, LANES - n_experts))[None, :]
    h, idx_pad, w_pad = _out_ln_route(merged, w_o.astype(BF16), x_all, ln1_g[None, :], ln1_b[None, :],
                                      wr_hi, wr_lo, b_r, alpha, n_experts)

    y = _moe_ln(h, idx_pad[:, :TOP_K], w_pad, w_gate, b_gate, w_up, b_up, w_down, b_down,
                ln2_g[None, :], ln2_b[None, :], alpha)

    def split_rows(a, h):
        return (a[:mp].reshape(nb_p, t_p, h, HEAD_DIM), a[mp:].reshape(nb_s, t_s, h, HEAD_DIM))

    logf = logf_pad[:, :hf]
    rows_p, rows_s = zip(split_rows(k_f, hf), split_rows(v_f, hf),
                         (logf[:mp].reshape(nb_p, t_p, hf), logf[mp:].reshape(nb_s, t_s, hf)),
                         split_rows(k_m, hm), split_rows(v_m, hm))
    return y, rows_p, rows_s


def _take_layer(a, l):
    return a.reshape(a.shape[1:]) if a.shape[0] == 1 else a[l]


def kernel(x_prompt, x_sample, cache_fox_k, cache_fox_v, cache_fox_logf, cache_moba_k, cache_moba_v, page_table, w_in, b_forget, w_br_fox, w_br_moba, w_o, ln1_g, ln1_b, w_router, b_router, w_gate, b_gate, w_up, b_up, w_down, b_down, ln2_g, ln2_b):
    depth = w_in.shape[0]
    alpha = (2.0 * depth) ** 0.25
    nb_p, t_p, d = x_prompt.shape
    nb_s, t_s, _ = x_sample.shape
    mp = nb_p * t_p
    x_all = jnp.concatenate([x_prompt.reshape(mp, d), x_sample.reshape(nb_s * t_s, d)], axis=0)
    rows_p, rows_s = [], []
    for l in range(depth):
        lw = tuple(_take_layer(w, l) for w in (w_in, b_forget, w_br_fox, w_br_moba, w_o, ln1_g, ln1_b, w_router, b_router,
                                               w_gate, b_gate, w_up, b_up, w_down, b_down, ln2_g, ln2_b))
        caches = tuple(_take_layer(c, l) for c in (cache_fox_k, cache_fox_v, cache_fox_logf, cache_moba_k, cache_moba_v))
        x_all, r_p, r_s = _layer(x_all, (nb_p, t_p, nb_s, t_s), caches, page_table, lw, alpha)
        rows_p.append(r_p)
        rows_s.append(r_s)

    def stack(rows, i):
        return jnp.stack([r[i] for r in rows])

    return (x_all[:mp].reshape(nb_p, t_p, d), x_all[mp:].reshape(nb_s, t_s, d),
            stack(rows_p, 0), stack(rows_p, 1), stack(rows_p, 2), stack(rows_p, 3), stack(rows_p, 4),
            stack(rows_s, 0), stack(rows_s, 1), stack(rows_s, 2), stack(rows_s, 3), stack(rows_s, 4))
```
